```python
import math
import jax, jax.numpy as jnp
from jax import lax
import numpy as np

D_MODEL = 1024
BATCH = 8
SEQ = 2048
DEPTH = 1

MIX_WIDTH = D_MODEL
DIFF_HEADS = 4
DIFF_QK_DIM = 64
DIFF_V_DIM = 2 * DIFF_QK_DIM
DIFF_WIDTH = DIFF_HEADS * DIFF_V_DIM
Q_BLOCK = 128
GLA_HEADS = 4
GLA_WIDTH = MIX_WIDTH - DIFF_WIDTH
GLA_V_DIM = GLA_WIDTH // GLA_HEADS
GLA_K_DIM = GLA_V_DIM // 2
GLA_GATE_RANK = 16
GLA_TAU = 16.0
GLA_CHUNK = 64
D_FF = 2816
CONV_WIDTH = 3
EPS = 1e-6

IN_SPLIT_SIZES = (
    DIFF_HEADS * 2 * DIFF_QK_DIM,
    DIFF_HEADS * 2 * DIFF_QK_DIM,
    DIFF_WIDTH,
    GLA_HEADS * GLA_K_DIM,
    GLA_HEADS * GLA_K_DIM,
    GLA_WIDTH,
    GLA_GATE_RANK,
    GLA_WIDTH,
)
IN_COLS = 3088

kernel_name = "hymba_diffattn_gla_convffn"


def rmsnorm(x, w):
    xf = x.astype(jnp.float32)
    y = xf * lax.rsqrt(jnp.mean(xf * xf, axis=-1, keepdims=True) + EPS)
    return (y * w.astype(jnp.float32)).astype(x.dtype)


def diff_attention(q, k, v, lam, subln_w, lambda_init):
    B, S, H, _, dq = q.shape
    dv = v.shape[-1]
    nb = S // Q_BLOCK
    scale = dq ** -0.5
    q_blocks = q.reshape(B, nb, Q_BLOCK, H, 2, dq).transpose(1, 0, 2, 3, 4, 5)
    kpos = jnp.arange(S)

    def one_block(args):
        q_blk, i = args
        s = jnp.einsum('bqhcd,bkhcd->bhcqk', q_blk, k).astype(jnp.float32) * scale
        qpos = i * Q_BLOCK + jnp.arange(Q_BLOCK)
        mask = kpos[None, :] <= qpos[:, None]
        s = jnp.where(mask, s, -jnp.inf)
        p = jax.nn.softmax(s, axis=-1)
        a = p[:, :, 0] - lam * p[:, :, 1]
        return jnp.einsum('bhqk,bkhe->bqhe', a.astype(v.dtype), v)

    o = lax.map(one_block, (q_blocks, jnp.arange(nb)))
    o = o.transpose(1, 0, 2, 3, 4).reshape(B, S, H, dv)
    o = rmsnorm(o, subln_w) * (1.0 - lambda_init)
    return o.reshape(B, S, H * dv)


def gla_chunked(q, k, v, log_g):
    B, S, H, dk = q.shape
    dv = v.shape[-1]
    C = GLA_CHUNK
    n = S // C

    def to_chunks(t):
        return t.reshape(B, n, C, H, t.shape[-1]).transpose(0, 3, 1, 2, 4)

    q, k, v, log_g = to_chunks(q), to_chunks(k), to_chunks(v), to_chunks(log_g)
    b = jnp.cumsum(log_g, axis=3)
    b_last = b[:, :, :, -1:, :]
    q_dec = q * jnp.exp(b)
    k_dec = k * jnp.exp(-b)
    k_to_end = k * jnp.exp(b_last - b)

    causal = jnp.tril(jnp.ones((C, C), dtype=bool))
    a = jnp.einsum('bhnik,bhnjk->bhnij', q_dec, k_dec)
    a = jnp.where(causal, a, 0.0)
    o_intra = jnp.einsum('bhnij,bhnjv->bhniv', a, v)

    chunk_state = jnp.einsum('bhnjk,bhnjv->bhnkv', k_to_end, v)
    decay = jnp.exp(b_last[:, :, :, 0, :])

    def step(state, inp):
        q_d, dec, cs = inp
        o = jnp.einsum('bhik,bhkv->bhiv', q_d, state)
        state = dec[..., None] * state + cs
        return state, o

    xs = (q_dec.transpose(2, 0, 1, 3, 4), decay.transpose(2, 0, 1, 3),
          chunk_state.transpose(2, 0, 1, 3, 4))
    _, o_inter = lax.scan(step, jnp.zeros((B, H, dk, dv), jnp.float32), xs)
    o = o_intra + o_inter.transpose(1, 2, 0, 3, 4)
    return o.transpose(0, 2, 3, 1, 4).reshape(B, S, H, dv)


def causal_depthwise_conv(u, w, b):
    C = u.shape[-1]
    y = lax.conv_general_dilated(
        u, w[:, None, :].astype(u.dtype), window_strides=(1,),
        padding=[(CONV_WIDTH - 1, 0)], dimension_numbers=('NWC', 'WIO', 'NWC'),
        feature_group_count=C)
    return y + b.astype(u.dtype)


def setup_inputs(seed: int = 0) -> dict:
    key = jax.random.key(seed)
    ks = jax.random.split(key, 20)
    f32 = jnp.float32
    nrm = lambda k, shape, s: jax.random.normal(k, shape, f32) * s
    L, D = DEPTH, D_MODEL
    return {
        "x": jax.random.normal(ks[0], (BATCH, SEQ, D), f32),
        "ln1_w": 1.0 + nrm(ks[1], (L, D), 0.02),
        "w_in": nrm(ks[2], (L, D, IN_COLS), D ** -0.5),
        "diff_lq1": nrm(ks[3], (L, DIFF_QK_DIM), 0.1),
        "diff_lk1": nrm(ks[4], (L, DIFF_QK_DIM), 0.1),
        "diff_lq2": nrm(ks[5], (L, DIFF_QK_DIM), 0.1),
        "diff_lk2": nrm(ks[6], (L, DIFF_QK_DIM), 0.1),
        "diff_subln_w": 1.0 + nrm(ks[7], (L, DIFF_V_DIM), 0.02),
        "gla_wg2": nrm(ks[8], (L, GLA_GATE_RANK, GLA_HEADS * GLA_K_DIM), GLA_GATE_RANK ** -0.5),
        "gla_bg": nrm(ks[9], (L, GLA_HEADS * GLA_K_DIM), 0.1),
        "gla_norm_w": 1.0 + nrm(ks[10], (L, GLA_V_DIM), 0.02),
        "w_out": nrm(ks[11], (L, MIX_WIDTH, D), MIX_WIDTH ** -0.5),
        "ln2_w": 1.0 + nrm(ks[12], (L, D), 0.02),
        "w_up": nrm(ks[13], (L, D, 2 * D_FF), D ** -0.5),
        "conv_w": nrm(ks[14], (L, CONV_WIDTH, 2 * D_FF), CONV_WIDTH ** -0.5),
        "conv_b": nrm(ks[15], (L, 2 * D_FF), 0.02),
        "w_down": nrm(ks[16], (L, D_FF, D), D_FF ** -0.5),
        "lnf_w": 1.0 + nrm(ks[17], (D,), 0.02),
    }


def reference(x, ln1_w, w_in, diff_lq1, diff_lk1, diff_lq2, diff_lk2, diff_subln_w,
              gla_wg2, gla_bg, gla_norm_w, w_out, ln2_w, w_up, conv_w, conv_b,
              w_down, lnf_w):
    B, S, _ = x.shape
    split_idx = [int(i) for i in np.cumsum(IN_SPLIT_SIZES)[:-1]]
    for l in range(DEPTH):
        lambda_init = 0.8 - 0.6 * math.exp(-0.3 * l)
        h = rmsnorm(x, ln1_w[l])
        proj = h @ w_in[l]
        d_q, d_k, d_v, g_q, g_k, g_v, g_r, g_o = jnp.split(proj, split_idx, axis=-1)

        lam = (jnp.exp(jnp.sum(diff_lq1[l].astype(jnp.float32) * diff_lk1[l].astype(jnp.float32)))
               - jnp.exp(jnp.sum(diff_lq2[l].astype(jnp.float32) * diff_lk2[l].astype(jnp.float32)))
               + lambda_init)
        o_diff = diff_attention(
            d_q.reshape(B, S, DIFF_HEADS, 2, DIFF_QK_DIM),
            d_k.reshape(B, S, DIFF_HEADS, 2, DIFF_QK_DIM),
            d_v.reshape(B, S, DIFF_HEADS, DIFF_V_DIM),
            lam, diff_subln_w[l], lambda_init)

        f32 = jnp.float32
        gq = g_q.astype(f32).reshape(B, S, GLA_HEADS, GLA_K_DIM) * GLA_K_DIM ** -0.5
        gk = g_k.astype(f32).reshape(B, S, GLA_HEADS, GLA_K_DIM)
        gv = g_v.astype(f32).reshape(B, S, GLA_HEADS, GLA_V_DIM)
        gate_logits = g_r.astype(f32) @ gla_wg2[l].astype(f32) + gla_bg[l].astype(f32)
        log_g = (jax.nn.log_sigmoid(gate_logits) / GLA_TAU).reshape(B, S, GLA_HEADS, GLA_K_DIM)
        o_gla = gla_chunked(gq, gk, gv, log_g).astype(x.dtype)
        o_gla = rmsnorm(o_gla, gla_norm_w[l]).reshape(B, S, GLA_WIDTH) * jax.nn.silu(g_o)

        mix = jnp.concatenate([o_diff, o_gla], axis=-1) @ w_out[l]
        x = x + mix

        h = rmsnorm(x, ln2_w[l])
        u = causal_depthwise_conv(h @ w_up[l], conv_w[l], conv_b[l])
        gate, val = jnp.split(u, 2, axis=-1)
        x = x + (jax.nn.silu(gate) * val) @ w_down[l]
    return rmsnorm(x, lnf_w)
```

```python
import functools
import math

import jax
import jax.numpy as jnp
from jax import lax
from jax.experimental import pallas as pl
from jax.experimental.pallas import tpu as pltpu

F32 = jnp.float32
BF16 = jnp.bfloat16

D_MODEL = 1024
DIFF_HEADS = 4
DIFF_QK_DIM = 64
DIFF_V_DIM = 128
DIFF_WIDTH = DIFF_HEADS * DIFF_V_DIM
GLA_HEADS = 4
GLA_K_DIM = 64
GLA_V_DIM = 128
GLA_KW = GLA_HEADS * GLA_K_DIM
GLA_WIDTH = GLA_HEADS * GLA_V_DIM
GLA_GATE_RANK = 16
GLA_TAU = 16.0
GLA_CHUNK = 64
D_FF = 2816
EPS = 1e-6

LANES = 128
VMEM_LIMIT = 56 * 1024 * 1024

TM_PROJ = 512
TM_FFN = 512
FFN_CHUNK = 256
TQ = 256
TK = 256


def _dot(a, b):
    return jnp.dot(a, b, preferred_element_type=F32)


def _dot_nt(a, b):
    return lax.dot_general(a, b, (((1,), (1,)), ((), ())), preferred_element_type=F32)


def _dot_tn(a, b):
    return lax.dot_general(a, b, (((0,), (0,)), ((), ())), preferred_element_type=F32)


def _rms(x, w):
    ms = jnp.mean(x * x, axis=-1, keepdims=True)
    return x * lax.rsqrt(ms + EPS) * w


def _const_spec(shape):
    nd = len(shape)
    return pl.BlockSpec(shape, lambda *_: (0,) * nd)


def _in_proj_kernel(x_ref, ln_ref, wa_ref, wg_ref, wr_ref, wg2_ref, bg_ref,
                    a_ref, g_ref, lg_ref):
    h = _rms(x_ref[...], ln_ref[...]).astype(BF16)
    a_ref[...] = _dot(h, wa_ref[...]).astype(BF16)
    g_ref[...] = _dot(h, wg_ref[...]).astype(BF16)
    r = _dot(h, wr_ref[...]).astype(BF16)
    logits = _dot(r, wg2_ref[...]) + bg_ref[...]
    ls = jnp.minimum(logits, 0.0) - jnp.log(1.0 + jnp.exp(-jnp.abs(logits)))
    lg_ref[...] = ls * (1.0 / GLA_TAU)


def _in_proj(x2, ln1_w, wa, wg, wr, wg2, bg):
    n = x2.shape[0]
    tm = TM_PROJ
    return pl.pallas_call(
        _in_proj_kernel,
        grid=(n // tm,),
        in_specs=[
            pl.BlockSpec((tm, D_MODEL), lambda i: (i, 0)),
            _const_spec((1, D_MODEL)),
            _const_spec(wa.shape),
            _const_spec(wg.shape),
            _const_spec(wr.shape),
            _const_spec(wg2.shape),
            _const_spec((1, GLA_KW)),
        ],
        out_specs=[
            pl.BlockSpec((tm, wa.shape[1]), lambda i: (i, 0)),
            pl.BlockSpec((tm, wg.shape[1]), lambda i: (i, 0)),
            pl.BlockSpec((tm, GLA_KW), lambda i: (i, 0)),
        ],
        out_shape=[
            jax.ShapeDtypeStruct((n, wa.shape[1]), BF16),
            jax.ShapeDtypeStruct((n, wg.shape[1]), BF16),
            jax.ShapeDtypeStruct((n, GLA_KW), F32),
        ],
        compiler_params=pltpu.CompilerParams(
            dimension_semantics=("arbitrary",), vmem_limit_bytes=VMEM_LIMIT),
        name="in_proj",
    )(x2, ln1_w, wa, wg, wr, wg2, bg)


def _diff_attn_kernel(lam_init, lq1_ref, lk1_ref, lq2_ref, lk2_ref, sw_ref,
                      q_ref, k_ref, v_ref, o_ref,
                      qt_ref, vt_ref, acc1_ref, acc2_ref):
    seq = q_ref.shape[1]
    nq = seq // TQ
    nk = seq // TK

    lam = (jnp.exp(jnp.sum(lq1_ref[...] * lk1_ref[...], axis=-1, keepdims=True))
           - jnp.exp(jnp.sum(lq2_ref[...] * lk2_ref[...], axis=-1, keepdims=True))
           + lam_init)

    for t in range(nq):
        qt_ref[t] = q_ref[0, t * TQ:(t + 1) * TQ, :].astype(F32).T.astype(BF16)
    for t in range(nk):
        vt_ref[t] = v_ref[0, t * TK:(t + 1) * TK, :].astype(F32).T.astype(BF16)

    row = lax.broadcasted_iota(jnp.int32, (2 * DIFF_QK_DIM, TQ), 0)
    krow = lax.broadcasted_iota(jnp.int32, (TK, TQ), 0)
    qcol = lax.broadcasted_iota(jnp.int32, (TK, TQ), 1)
    diag_bias = jnp.where(krow <= qcol, 0.0, -jnp.inf).astype(F32)

    def one_map(s, m, l, acc_ref, vt):
        m_new = jnp.maximum(m, jnp.max(s, axis=0, keepdims=True))
        alpha = jnp.exp(m - m_new)
        p = jnp.exp(s - m_new)
        l_new = alpha * l + jnp.sum(p, axis=0, keepdims=True)
        acc_ref[...] = alpha * acc_ref[...] + _dot(vt, p.astype(BF16))
        return m_new, l_new

    def q_tile(i, carry):
        qt = qt_ref[i]
        zero = jnp.zeros_like(qt)
        q1t = jnp.where(row < DIFF_QK_DIM, qt, zero)
        q2t = jnp.where(row >= DIFF_QK_DIM, qt, zero)
        acc1_ref[...] = jnp.zeros_like(acc1_ref)
        acc2_ref[...] = jnp.zeros_like(acc2_ref)
        neg = jnp.full((1, TQ), -jnp.inf, F32)
        zl = jnp.zeros((1, TQ), F32)

        def k_tile(j, st):
            m1, l1, m2, l2 = st
            kj = k_ref[0, pl.ds(pl.multiple_of(j * TK, TK), TK), :]
            vt = vt_ref[j]
            m1, l1 = one_map(_dot(kj, q1t), m1, l1, acc1_ref, vt)
            m2, l2 = one_map(_dot(kj, q2t), m2, l2, acc2_ref, vt)
            return m1, l1, m2, l2

        st = lax.fori_loop(0, i, k_tile, (neg, zl, neg, zl))
        m1, l1, m2, l2 = st
        kj = k_ref[0, pl.ds(pl.multiple_of(i * TK, TK), TK), :]
        vt = vt_ref[i]
        m1, l1 = one_map(_dot(kj, q1t) + diag_bias, m1, l1, acc1_ref, vt)
        m2, l2 = one_map(_dot(kj, q2t) + diag_bias, m2, l2, acc2_ref, vt)

        ot = acc1_ref[...] / l1 - lam * (acc2_ref[...] / l2)
        ms = jnp.mean(ot * ot, axis=0, keepdims=True)
        ot = ot * lax.rsqrt(ms + EPS) * sw_ref[...] * (1.0 - lam_init)
        o_ref[0, pl.ds(pl.multiple_of(i * TQ, TQ), TQ), :] = ot.T.astype(BF16)
        return carry

    lax.fori_loop(0, nq, q_tile, 0)


def _diff_attn(a3, lq1, lk1, lq2, lk2, subln_col, lam_init):
    b, seq, _ = a3.shape
    hd = 2 * DIFF_QK_DIM
    vec = _const_spec((1, DIFF_QK_DIM))
    return pl.pallas_call(
        functools.partial(_diff_attn_kernel, lam_init),
        grid=(b, DIFF_HEADS),
        in_specs=[
            vec, vec, vec, vec,
            _const_spec((DIFF_V_DIM, 1)),
            pl.BlockSpec((1, seq, hd), lambda bi, h: (bi, 0, h)),
            pl.BlockSpec((1, seq, hd), lambda bi, h: (bi, 0, DIFF_HEADS + h)),
            pl.BlockSpec((1, seq, DIFF_V_DIM), lambda bi, h: (bi, 0, 2 * DIFF_HEADS + h)),
        ],
        out_specs=pl.BlockSpec((1, seq, DIFF_V_DIM), lambda bi, h: (bi, 0, h)),
        out_shape=jax.ShapeDtypeStruct((b, seq, DIFF_WIDTH), BF16),
        scratch_shapes=[
            pltpu.VMEM((seq // TQ, hd, TQ), BF16),
            pltpu.VMEM((seq // TK, DIFF_V_DIM, TK), BF16),
            pltpu.VMEM((DIFF_V_DIM, TQ), F32),
            pltpu.VMEM((DIFF_V_DIM, TQ), F32),
        ],
        compiler_params=pltpu.CompilerParams(
            dimension_semantics=("arbitrary", "arbitrary"), vmem_limit_bytes=VMEM_LIMIT),
        name="diff_attn",
    )(lq1, lk1, lq2, lk2, subln_col, a3, a3, a3)


def _gla_kernel(g_ref, lg_ref, nw_ref, o_ref, s_ref):
    c = GLA_CHUNK
    seq = g_ref.shape[1]
    kw, vw = GLA_KW, GLA_WIDTH
    s_ref[...] = jnp.zeros_like(s_ref)

    ri = lax.broadcasted_iota(jnp.int32, (c, c), 0)
    ci = lax.broadcasted_iota(jnp.int32, (c, c), 1)
    causal = ri >= ci
    tri = causal.astype(BF16)
    ones_cl = jnp.ones((c, LANES), BF16)
    lane_head = lax.broadcasted_iota(jnp.int32, (c, kw), 1) // GLA_K_DIM
    srow_head = lax.broadcasted_iota(jnp.int32, (kw, vw), 0) // GLA_K_DIM
    scol_head = lax.broadcasted_iota(jnp.int32, (kw, vw), 1) // GLA_V_DIM
    blockdiag = srow_head == scol_head

    def chunk(n, carry):
        r0 = pl.multiple_of(n * c, c)
        lg = lg_ref[0, pl.ds(r0, c), :]
        hi = lg.astype(BF16)
        lo = (lg - hi.astype(F32)).astype(BF16)
        b = _dot(tri, hi) + _dot(tri, lo)
        bl = b[c - 1:c, :]
        g = g_ref[0, pl.ds(r0, c), :]
        q = g[:, 0:kw].astype(F32)
        k = g[:, kw:2 * kw].astype(F32)
        v = g[:, 2 * kw:2 * kw + vw]
        go = g[:, 2 * kw + vw:2 * kw + 2 * vw].astype(F32)
        qd = (q * jnp.exp(b)).astype(BF16)
        kd = (k * jnp.exp(-b)).astype(BF16)
        ke = (k * jnp.exp(bl - b)).astype(BF16)

        o_inter = _dot(qd, s_ref[...].astype(BF16))
        zero = jnp.zeros_like(qd)
        outs = []
        for h in range(GLA_HEADS):
            qh = jnp.where(lane_head == h, qd, zero)
            a = jnp.where(causal, _dot_nt(qh, kd), 0.0)
            outs.append(_dot(a.astype(BF16), v[:, h * GLA_V_DIM:(h + 1) * GLA_V_DIM]))
        o = jnp.concatenate(outs, axis=1) + o_inter

        cs = jnp.where(blockdiag, _dot_tn(ke, v), 0.0)
        dcol = jnp.exp(_dot_tn(hi, ones_cl) + _dot_tn(lo, ones_cl))
        s_ref[...] = jnp.concatenate([dcol] * (vw // LANES), axis=1) * s_ref[...] + cs

        nw = nw_ref[...]
        ys = []
        for h in range(GLA_HEADS):
            sl = slice(h * GLA_V_DIM, (h + 1) * GLA_V_DIM)
            gate = go[:, sl]
            ys.append(_rms(o[:, sl], nw) * (gate / (1.0 + jnp.exp(-gate))))
        o_ref[0, pl.ds(r0, c), :] = jnp.concatenate(ys, axis=1).astype(BF16)
        return carry

    lax.fori_loop(0, seq // c, chunk, 0)


def _gla(g3, lg3, norm_w):
    b, seq, gw = g3.shape
    return pl.pallas_call(
        _gla_kernel,
        grid=(b,),
        in_specs=[
            pl.BlockSpec((1, seq, gw), lambda bi: (bi, 0, 0)),
            pl.BlockSpec((1, seq, GLA_KW), lambda bi: (bi, 0, 0)),
            _const_spec((1, GLA_V_DIM)),
        ],
        out_specs=pl.BlockSpec((1, seq, GLA_WIDTH), lambda bi: (bi, 0, 0)),
        out_shape=jax.ShapeDtypeStruct((b, seq, GLA_WIDTH), BF16),
        scratch_shapes=[pltpu.VMEM((GLA_KW, GLA_WIDTH), F32)],
        compiler_params=pltpu.CompilerParams(
            dimension_semantics=("arbitrary",), vmem_limit_bytes=VMEM_LIMIT),
        name="gla",
    )(g3, lg3, norm_w)


def _out_proj_kernel(x_ref, od_ref, og_ref, wd_ref, wg_ref, ln_ref, x1_ref, h_ref):
    x1 = x_ref[...] + _dot(od_ref[...], wd_ref[...]) + _dot(og_ref[...], wg_ref[...])
    x1_ref[...] = x1
    h_ref[...] = _rms(x1, ln_ref[...]).astype(BF16)


def _out_proj(x2, od, og, wd, wg, ln2_w):
    n = x2.shape[0]
    tm = TM_PROJ
    tok = lambda w: pl.BlockSpec((tm, w), lambda i: (i, 0))
    return pl.pallas_call(
        _out_proj_kernel,
        grid=(n // tm,),
        in_specs=[tok(D_MODEL), tok(DIFF_WIDTH), tok(GLA_WIDTH),
                  _const_spec(wd.shape), _const_spec(wg.shape), _const_spec((1, D_MODEL))],
        out_specs=[tok(D_MODEL), tok(D_MODEL)],
        out_shape=[jax.ShapeDtypeStruct((n, D_MODEL), F32),
                   jax.ShapeDtypeStruct((n, D_MODEL), BF16)],
        compiler_params=pltpu.CompilerParams(
            dimension_semantics=("arbitrary",), vmem_limit_bytes=VMEM_LIMIT),
        name="out_proj",
    )(x2, od, og, wd, wg, ln2_w)


def _ffn_kernel(tiles_per_seq, h_ref, x1_ref, wup_ref, cw_ref, cb_ref, wdn_ref, lnf_ref,
                o_ref, u_ref, carry_ref, acc_ref):
    tm = h_ref.shape[0]
    halo = 8
    nch = D_FF // FFN_CHUNK

    @pl.when(pl.program_id(0) % tiles_per_seq == 0)
    def _():
        carry_ref[...] = jnp.zeros_like(carry_ref)

    h = h_ref[...]
    acc_ref[...] = jnp.zeros_like(acc_ref)
    for c in range(nch):
        ys = []
        for half in range(2):
            col = half * D_FF + c * FFN_CHUNK
            cs = slice(col, col + FFN_CHUNK)
            u = _dot(h, wup_ref[:, cs])
            u_ref[0:halo, :] = carry_ref[half, c]
            u_ref[halo:halo + tm, :] = u
            carry_ref[half, c] = u[tm - halo:tm, :]
            w = cw_ref[:, cs]
            y = (u * w[2:3, :]
                 + u_ref[halo - 1:halo - 1 + tm, :] * w[1:2, :]
                 + u_ref[halo - 2:halo - 2 + tm, :] * w[0:1, :]
                 + cb_ref[:, cs])
            ys.append(y)
        gate, val = ys
        act = (gate / (1.0 + jnp.exp(-gate)) * val).astype(BF16)
        acc_ref[...] += _dot(act, wdn_ref[c * FFN_CHUNK:(c + 1) * FFN_CHUNK, :])
    o_ref[...] = _rms(x1_ref[...] + acc_ref[...], lnf_ref[...])


def _ffn(h2, x1, wup, conv_w, conv_b, wdn, lnf_w, seq):
    n = h2.shape[0]
    tm = TM_FFN
    tok = lambda: pl.BlockSpec((tm, D_MODEL), lambda i: (i, 0))
    return pl.pallas_call(
        functools.partial(_ffn_kernel, seq // tm),
        grid=(n // tm,),
        in_specs=[tok(), tok(),
                  _const_spec(wup.shape), _const_spec(conv_w.shape), _const_spec(conv_b.shape),
                  _const_spec(wdn.shape), _const_spec((1, D_MODEL))],
        out_specs=tok(),
        out_shape=jax.ShapeDtypeStruct((n, D_MODEL), F32),
        scratch_shapes=[
            pltpu.VMEM((tm + 8, FFN_CHUNK), F32),
            pltpu.VMEM((2, D_FF // FFN_CHUNK, 8, FFN_CHUNK), F32),
            pltpu.VMEM((tm, D_MODEL), F32),
        ],
        compiler_params=pltpu.CompilerParams(
            dimension_semantics=("arbitrary",), vmem_limit_bytes=VMEM_LIMIT),
        name="ffn",
    )(h2, x1, wup, conv_w, conv_b, wdn, lnf_w)


def kernel(x, ln1_w, w_in, diff_lq1, diff_lk1, diff_lq2, diff_lk2, diff_subln_w,
           gla_wg2, gla_bg, gla_norm_w, w_out, ln2_w, w_up, conv_w, conv_b,
           w_down, lnf_w):
    b, seq, d = x.shape
    n = b * seq
    depth = w_in.shape[0]
    assert depth == 1, depth
    x2 = x.reshape(n, d)
    qk_w = DIFF_HEADS * 2 * DIFF_QK_DIM
    c0 = 2 * qk_w + DIFF_WIDTH
    c1 = c0 + 2 * GLA_KW + GLA_WIDTH
    c2 = c1 + GLA_GATE_RANK
    for l in range(depth):
        lam_init = 0.8 - 0.6 * math.exp(-0.3 * l)
        w = w_in[l]
        wa = jnp.concatenate([w[:, :qk_w] * DIFF_QK_DIM ** -0.5, w[:, qk_w:c0]], axis=1).astype(BF16)
        wg = jnp.concatenate([w[:, c0:c0 + GLA_KW] * GLA_K_DIM ** -0.5,
                              w[:, c0 + GLA_KW:c1], w[:, c2:]], axis=1).astype(BF16)
        wr = jnp.pad(w[:, c1:c2], ((0, 0), (0, LANES - GLA_GATE_RANK))).astype(BF16)
        wg2 = jnp.pad(gla_wg2[l], ((0, LANES - GLA_GATE_RANK), (0, 0))).astype(BF16)

        a, g, lg = _in_proj(x2, ln1_w[l][None, :], wa, wg, wr, wg2, gla_bg[l][None, :])
        o_diff = _diff_attn(a.reshape(b, seq, -1),
                            diff_lq1[l][None, :], diff_lk1[l][None, :],
                            diff_lq2[l][None, :], diff_lk2[l][None, :],
                            diff_subln_w[l][:, None], lam_init)
        o_gla = _gla(g.reshape(b, seq, -1), lg.reshape(b, seq, -1), gla_norm_w[l][None, :])

        wo = w_out[l].astype(BF16)
        x1, h2 = _out_proj(x2, o_diff.reshape(n, -1), o_gla.reshape(n, -1),
                           wo[:DIFF_WIDTH], wo[DIFF_WIDTH:], ln2_w[l][None, :])
        x2 = _ffn(h2, x1, w_up[l].astype(BF16), conv_w[l], conv_b[l][None, :],
                  w_down[l].astype(BF16), lnf_w[None, :], seq)
    return x2.reshape(b, seq, d)
```

```python
import functools
import math

import jax
import jax.numpy as jnp
from jax import lax
from jax.experimental import pallas as pl
from jax.experimental.pallas import tpu as pltpu

F32 = jnp.float32
BF16 = jnp.bfloat16

D_MODEL = 1024
DIFF_HEADS = 4
DIFF_QK_DIM = 64
DIFF_V_DIM = 128
DIFF_WIDTH = DIFF_HEADS * DIFF_V_DIM
GLA_HEADS = 4
GLA_K_DIM = 64
GLA_V_DIM = 128
GLA_KW = GLA_HEADS * GLA_K_DIM
GLA_WIDTH = GLA_HEADS * GLA_V_DIM
GLA_GATE_RANK = 16
GLA_TAU = 16.0
GLA_CHUNK = 64
D_FF = 2816
EPS = 1e-6

LANES = 128
VMEM_LIMIT = 56 * 1024 * 1024

TM_PROJ = 512
TM_FFN = 512
FFN_CHUNK = 256
TQ = 256
TK = 256
ROW_BLK = 64


def _dot(a, b):
    return jnp.dot(a, b, preferred_element_type=F32)


def _dot_nt(a, b):
    return lax.dot_general(a, b, (((1,), (1,)), ((), ())), preferred_element_type=F32)


def _dot_tn(a, b):
    return lax.dot_general(a, b, (((0,), (0,)), ((), ())), preferred_element_type=F32)


def _rms(x, w):
    ms = jnp.mean(x * x, axis=-1, keepdims=True)
    return x * lax.rsqrt(ms + EPS) * w


def _const_spec(shape):
    nd = len(shape)
    return pl.BlockSpec(shape, lambda *_: (0,) * nd)


def _in_proj_kernel(x_ref, ln_ref, wa_ref, wg_ref, wr_ref, wg2_ref, bg_ref,
                    a_ref, g_ref, lg_ref):
    h = _rms(x_ref[...], ln_ref[...]).astype(BF16)
    a_ref[...] = _dot(h, wa_ref[...]).astype(BF16)
    g_ref[...] = _dot(h, wg_ref[...]).astype(BF16)
    r = _dot(h, wr_ref[...]).astype(BF16)
    logits = _dot(r, wg2_ref[...]) + bg_ref[...]
    ls = jnp.minimum(logits, 0.0) - jnp.log(1.0 + jnp.exp(-jnp.abs(logits)))
    lg_ref[...] = ls * (1.0 / GLA_TAU)


def _in_proj(x2, ln1_w, wa, wg, wr, wg2, bg):
    n = x2.shape[0]
    tm = TM_PROJ
    return pl.pallas_call(
        _in_proj_kernel,
        grid=(n // tm,),
        in_specs=[
            pl.BlockSpec((tm, D_MODEL), lambda i: (i, 0)),
            _const_spec((1, D_MODEL)),
            _const_spec(wa.shape),
            _const_spec(wg.shape),
            _const_spec(wr.shape),
            _const_spec(wg2.shape),
            _const_spec((1, GLA_KW)),
        ],
        out_specs=[
            pl.BlockSpec((tm, wa.shape[1]), lambda i: (i, 0)),
            pl.BlockSpec((tm, wg.shape[1]), lambda i: (i, 0)),
            pl.BlockSpec((tm, GLA_KW), lambda i: (i, 0)),
        ],
        out_shape=[
            jax.ShapeDtypeStruct((n, wa.shape[1]), BF16),
            jax.ShapeDtypeStruct((n, wg.shape[1]), BF16),
            jax.ShapeDtypeStruct((n, GLA_KW), F32),
        ],
        compiler_params=pltpu.CompilerParams(
            dimension_semantics=("arbitrary",), vmem_limit_bytes=VMEM_LIMIT),
        name="in_proj",
    )(x2, ln1_w, wa, wg, wr, wg2, bg)


def _diff_attn_kernel(lam_init, lq1_ref, lk1_ref, lq2_ref, lk2_ref, sw_ref,
                      q_ref, k_ref, v_ref, o_ref,
                      vt_ref, bias_ref, s_ref4, p_ref4):
    assert TQ == TK
    seq = q_ref.shape[1]
    nq = seq // TQ
    nk = seq // TK

    lam = (jnp.exp(jnp.sum(lq1_ref[...] * lk1_ref[...], axis=-1, keepdims=True))
           - jnp.exp(jnp.sum(lq2_ref[...] * lk2_ref[...], axis=-1, keepdims=True))
           + lam_init)

    for t in range(nk):
        vt_ref[:, t * TK:(t + 1) * TK] = v_ref[0, t * TK:(t + 1) * TK, :].astype(F32).T.astype(BF16)

    row = lax.broadcasted_iota(jnp.int32, (2 * DIFF_QK_DIM, TQ), 0)
    krow = lax.broadcasted_iota(jnp.int32, (TQ, 2 * TQ), 0)
    qcol = lax.broadcasted_iota(jnp.int32, (TQ, 2 * TQ), 1) % TQ
    bias_ref[...] = jnp.where(krow <= qcol, 0.0, -jnp.inf).astype(F32)

    def fold8(x, op):
        parts = [x[a * 8:(a + 1) * 8, :] for a in range(x.shape[0] // 8)]
        while len(parts) > 1:
            parts = [op(parts[a], parts[a + 1]) for a in range(0, len(parts), 2)]
        return parts[0]

    def scores(i):
        nkeys = (i + 1) * TQ
        qt = q_ref[0, i * TQ:(i + 1) * TQ, :].astype(F32).T.astype(BF16)
        zero = jnp.zeros_like(qt)
        q12t = jnp.concatenate([jnp.where(row < DIFF_QK_DIM, qt, zero),
                                jnp.where(row >= DIFF_QK_DIM, qt, zero)], axis=1)
        s_ref4[i % 2, 0:nkeys, :] = _dot(k_ref[0, 0:nkeys, :], q12t)

    def softmax(i):
        nblk = (i + 1) * TQ // ROW_BLK

        def block(r):
            blk = s_ref4[i % 2, r * ROW_BLK:(r + 1) * ROW_BLK, :]
            d = r * ROW_BLK - i * TQ
            if d >= 0:
                blk = blk + bias_ref[d:d + ROW_BLK, :]
            return blk

        m8 = fold8(block(0), jnp.maximum)
        for r in range(1, nblk):
            m8 = jnp.maximum(m8, fold8(block(r), jnp.maximum))
        m = jnp.max(m8, axis=0, keepdims=True)
        mb = jnp.broadcast_to(m, (ROW_BLK, 2 * TQ))
        l8 = jnp.zeros((8, 2 * TQ), F32)
        for r in range(nblk):
            p = jnp.exp(block(r) - mb)
            l8 = l8 + fold8(p, jnp.add)
            p_ref4[i % 2, r * ROW_BLK:(r + 1) * ROW_BLK, :] = p.astype(BF16)
        return jnp.sum(l8, axis=0, keepdims=True)

    def values(i, l):
        nkeys = (i + 1) * TQ
        acc = _dot(vt_ref[:, 0:nkeys], p_ref4[i % 2, 0:nkeys, :]) / l
        ot = acc[:, :TQ] - lam * acc[:, TQ:]
        ms = jnp.mean(ot * ot, axis=0, keepdims=True)
        ot = ot * lax.rsqrt(ms + EPS) * sw_ref[...] * (1.0 - lam_init)
        o_ref[0, i * TQ:(i + 1) * TQ, :] = ot.T.astype(BF16)

    scores(0)
    sums = {}
    for i in range(nq):
        if i > 0:
            values(i - 1, sums.pop(i - 1))
        if i + 1 < nq:
            scores(i + 1)
        sums[i] = softmax(i)
    values(nq - 1, sums.pop(nq - 1))


def _diff_attn(a3, lq1, lk1, lq2, lk2, subln_col, lam_init):
    b, seq, _ = a3.shape
    hd = 2 * DIFF_QK_DIM
    vec = _const_spec((1, DIFF_QK_DIM))
    return pl.pallas_call(
        functools.partial(_diff_attn_kernel, lam_init),
        grid=(b, DIFF_HEADS),
        in_specs=[
            vec, vec, vec, vec,
            _const_spec((DIFF_V_DIM, 1)),
            pl.BlockSpec((1, seq, hd), lambda bi, h: (bi, 0, h)),
            pl.BlockSpec((1, seq, hd), lambda bi, h: (bi, 0, DIFF_HEADS + h)),
            pl.BlockSpec((1, seq, DIFF_V_DIM), lambda bi, h: (bi, 0, 2 * DIFF_HEADS + h)),
        ],
        out_specs=pl.BlockSpec((1, seq, DIFF_V_DIM), lambda bi, h: (bi, 0, h)),
        out_shape=jax.ShapeDtypeStruct((b, seq, DIFF_WIDTH), BF16),
        scratch_shapes=[
            pltpu.VMEM((DIFF_V_DIM, seq), BF16),
            pltpu.VMEM((TQ, 2 * TQ), F32),
            pltpu.VMEM((2, seq, 2 * TQ), F32),
            pltpu.VMEM((2, seq, 2 * TQ), BF16),
        ],
        compiler_params=pltpu.CompilerParams(
            dimension_semantics=("arbitrary", "arbitrary"), vmem_limit_bytes=VMEM_LIMIT),
        name="diff_attn",
    )(lq1, lk1, lq2, lk2, subln_col, a3, a3, a3)


def _gla_kernel(g_ref, lg_ref, nw_ref, o_ref, s_ref):
    c = GLA_CHUNK
    seq = g_ref.shape[1]
    kw, vw = GLA_KW, GLA_WIDTH
    s_ref[...] = jnp.zeros_like(s_ref)

    ri = lax.broadcasted_iota(jnp.int32, (c, c), 0)
    ci = lax.broadcasted_iota(jnp.int32, (c, c), 1)
    causal = ri >= ci
    tri = causal.astype(BF16)
    ones_cl = jnp.ones((c, LANES), BF16)
    lane_head = lax.broadcasted_iota(jnp.int32, (c, kw), 1) // GLA_K_DIM
    srow_head = lax.broadcasted_iota(jnp.int32, (kw, vw), 0) // GLA_K_DIM
    scol_head = lax.broadcasted_iota(jnp.int32, (kw, vw), 1) // GLA_V_DIM
    blockdiag = srow_head == scol_head

    def chunk(n, carry):
        r0 = pl.multiple_of(n * c, c)
        lg = lg_ref[0, pl.ds(r0, c), :]
        hi = lg.astype(BF16)
        lo = (lg - hi.astype(F32)).astype(BF16)
        b = _dot(tri, hi) + _dot(tri, lo)
        bl = b[c - 1:c, :]
        g = g_ref[0, pl.ds(r0, c), :]
        q = g[:, 0:kw].astype(F32)
        k = g[:, kw:2 * kw].astype(F32)
        v = g[:, 2 * kw:2 * kw + vw]
        go = g[:, 2 * kw + vw:2 * kw + 2 * vw].astype(F32)
        qd = (q * jnp.exp(b)).astype(BF16)
        kd = (k * jnp.exp(-b)).astype(BF16)
        ke = (k * jnp.exp(bl - b)).astype(BF16)

        o_inter = _dot(qd, s_ref[...].astype(BF16))
        zero = jnp.zeros_like(qd)
        outs = []
        for h in range(GLA_HEADS):
            qh = jnp.where(lane_head == h, qd, zero)
            a = jnp.where(causal, _dot_nt(qh, kd), 0.0)
            outs.append(_dot(a.astype(BF16), v[:, h * GLA_V_DIM:(h + 1) * GLA_V_DIM]))
        o = jnp.concatenate(outs, axis=1) + o_inter

        cs = jnp.where(blockdiag, _dot_tn(ke, v), 0.0)
        dcol = jnp.exp(_dot_tn(hi, ones_cl) + _dot_tn(lo, ones_cl))
        s_ref[...] = jnp.concatenate([dcol] * (vw // LANES), axis=1) * s_ref[...] + cs

        nw = nw_ref[...]
        ys = []
        for h in range(GLA_HEADS):
            sl = slice(h * GLA_V_DIM, (h + 1) * GLA_V_DIM)
            gate = go[:, sl]
            ys.append(_rms(o[:, sl], nw) * (gate / (1.0 + jnp.exp(-gate))))
        o_ref[0, pl.ds(r0, c), :] = jnp.concatenate(ys, axis=1).astype(BF16)
        return carry

    lax.fori_loop(0, seq // c, chunk, 0)


def _gla(g3, lg3, norm_w):
    b, seq, gw = g3.shape
    return pl.pallas_call(
        _gla_kernel,
        grid=(b,),
        in_specs=[
            pl.BlockSpec((1, seq, gw), lambda bi: (bi, 0, 0)),
            pl.BlockSpec((1, seq, GLA_KW), lambda bi: (bi, 0, 0)),
            _const_spec((1, GLA_V_DIM)),
        ],
        out_specs=pl.BlockSpec((1, seq, GLA_WIDTH), lambda bi: (bi, 0, 0)),
        out_shape=jax.ShapeDtypeStruct((b, seq, GLA_WIDTH), BF16),
        scratch_shapes=[pltpu.VMEM((GLA_KW, GLA_WIDTH), F32)],
        compiler_params=pltpu.CompilerParams(
            dimension_semantics=("arbitrary",), vmem_limit_bytes=VMEM_LIMIT),
        name="gla",
    )(g3, lg3, norm_w)


def _out_proj_kernel(x_ref, od_ref, og_ref, wd_ref, wg_ref, ln_ref, x1_ref, h_ref):
    x1 = x_ref[...] + _dot(od_ref[...], wd_ref[...]) + _dot(og_ref[...], wg_ref[...])
    x1_ref[...] = x1
    h_ref[...] = _rms(x1, ln_ref[...]).astype(BF16)


def _out_proj(x2, od, og, wd, wg, ln2_w):
    n = x2.shape[0]
    tm = TM_PROJ
    tok = lambda w: pl.BlockSpec((tm, w), lambda i: (i, 0))
    return pl.pallas_call(
        _out_proj_kernel,
        grid=(n // tm,),
        in_specs=[tok(D_MODEL), tok(DIFF_WIDTH), tok(GLA_WIDTH),
                  _const_spec(wd.shape), _const_spec(wg.shape), _const_spec((1, D_MODEL))],
        out_specs=[tok(D_MODEL), tok(D_MODEL)],
        out_shape=[jax.ShapeDtypeStruct((n, D_MODEL), F32),
                   jax.ShapeDtypeStruct((n, D_MODEL), BF16)],
        compiler_params=pltpu.CompilerParams(
            dimension_semantics=("arbitrary",), vmem_limit_bytes=VMEM_LIMIT),
        name="out_proj",
    )(x2, od, og, wd, wg, ln2_w)


def _ffn_kernel(tiles_per_seq, h_ref, x1_ref, wup_ref, cw_ref, cb_ref, wdn_ref, lnf_ref,
                o_ref, u_ref, carry_ref, acc_ref):
    tm = h_ref.shape[0]
    halo = 8
    nch = D_FF // FFN_CHUNK

    @pl.when(pl.program_id(0) % tiles_per_seq == 0)
    def _():
        carry_ref[...] = jnp.zeros_like(carry_ref)

    h = h_ref[...]
    acc_ref[...] = jnp.zeros_like(acc_ref)
    for c in range(nch):
        ys = []
        for half in range(2):
            col = half * D_FF + c * FFN_CHUNK
            cs = slice(col, col + FFN_CHUNK)
            u = _dot(h, wup_ref[:, cs])
            u_ref[0:halo, :] = carry_ref[half, c]
            u_ref[halo:halo + tm, :] = u
            carry_ref[half, c] = u[tm - halo:tm, :]
            w = cw_ref[:, cs]
            y = (u * w[2:3, :]
                 + u_ref[halo - 1:halo - 1 + tm, :] * w[1:2, :]
                 + u_ref[halo - 2:halo - 2 + tm, :] * w[0:1, :]
                 + cb_ref[:, cs])
            ys.append(y)
        gate, val = ys
        act = (gate / (1.0 + jnp.exp(-gate)) * val).astype(BF16)
        acc_ref[...] += _dot(act, wdn_ref[c * FFN_CHUNK:(c + 1) * FFN_CHUNK, :])
    o_ref[...] = _rms(x1_ref[...] + acc_ref[...], lnf_ref[...])


def _ffn(h2, x1, wup, conv_w, conv_b, wdn, lnf_w, seq):
    n = h2.shape[0]
    tm = TM_FFN
    tok = lambda: pl.BlockSpec((tm, D_MODEL), lambda i: (i, 0))
    return pl.pallas_call(
        functools.partial(_ffn_kernel, seq // tm),
        grid=(n // tm,),
        in_specs=[tok(), tok(),
                  _const_spec(wup.shape), _const_spec(conv_w.shape), _const_spec(conv_b.shape),
                  _const_spec(wdn.shape), _const_spec((1, D_MODEL))],
        out_specs=tok(),
        out_shape=jax.ShapeDtypeStruct((n, D_MODEL), F32),
        scratch_shapes=[
            pltpu.VMEM((tm + 8, FFN_CHUNK), F32),
            pltpu.VMEM((2, D_FF // FFN_CHUNK, 8, FFN_CHUNK), F32),
            pltpu.VMEM((tm, D_MODEL), F32),
        ],
        compiler_params=pltpu.CompilerParams(
            dimension_semantics=("arbitrary",), vmem_limit_bytes=VMEM_LIMIT),
        name="ffn",
    )(h2, x1, wup, conv_w, conv_b, wdn, lnf_w)


def kernel(x, ln1_w, w_in, diff_lq1, diff_lk1, diff_lq2, diff_lk2, diff_subln_w,
           gla_wg2, gla_bg, gla_norm_w, w_out, ln2_w, w_up, conv_w, conv_b,
           w_down, lnf_w):
    b, seq, d = x.shape
    n = b * seq
    depth = w_in.shape[0]
    assert depth == 1, depth
    x2 = x.reshape(n, d)
    qk_w = DIFF_HEADS * 2 * DIFF_QK_DIM
    c0 = 2 * qk_w + DIFF_WIDTH
    c1 = c0 + 2 * GLA_KW + GLA_WIDTH
    c2 = c1 + GLA_GATE_RANK
    for l in range(depth):
        lam_init = 0.8 - 0.6 * math.exp(-0.3 * l)
        w = w_in[l]
        wa = jnp.concatenate([w[:, :qk_w] * DIFF_QK_DIM ** -0.5, w[:, qk_w:c0]], axis=1).astype(BF16)
        wg = jnp.concatenate([w[:, c0:c0 + GLA_KW] * GLA_K_DIM ** -0.5,
                              w[:, c0 + GLA_KW:c1], w[:, c2:]], axis=1).astype(BF16)
        wr = jnp.pad(w[:, c1:c2], ((0, 0), (0, LANES - GLA_GATE_RANK))).astype(BF16)
        wg2 = jnp.pad(gla_wg2[l], ((0, LANES - GLA_GATE_RANK), (0, 0))).astype(BF16)

        a, g, lg = _in_proj(x2, ln1_w[l][None, :], wa, wg, wr, wg2, gla_bg[l][None, :])
        o_diff = _diff_attn(a.reshape(b, seq, -1),
                            diff_lq1[l][None, :], diff_lk1[l][None, :],
                            diff_lq2[l][None, :], diff_lk2[l][None, :],
                            diff_subln_w[l][:, None], lam_init)
        o_gla = _gla(g.reshape(b, seq, -1), lg.reshape(b, seq, -1), gla_norm_w[l][None, :])

        wo = w_out[l].astype(BF16)
        x1, h2 = _out_proj(x2, o_diff.reshape(n, -1), o_gla.reshape(n, -1),
                           wo[:DIFF_WIDTH], wo[DIFF_WIDTH:], ln2_w[l][None, :])
        x2 = _ffn(h2, x1, w_up[l].astype(BF16), conv_w[l], conv_b[l][None, :],
                  w_down[l].astype(BF16), lnf_w[None, :], seq)
    return x2.reshape(b, seq, d)
```

```python
import functools
import math

import jax
import jax.numpy as jnp
from jax import lax
from jax.experimental import pallas as pl
from jax.experimental.pallas import tpu as pltpu

F32 = jnp.float32
BF16 = jnp.bfloat16

D_MODEL = 1024
DIFF_HEADS = 4
DIFF_QK_DIM = 64
DIFF_V_DIM = 128
DIFF_WIDTH = DIFF_HEADS * DIFF_V_DIM
GLA_HEADS = 4
GLA_K_DIM = 64
GLA_V_DIM = 128
GLA_KW = GLA_HEADS * GLA_K_DIM
GLA_WIDTH = GLA_HEADS * GLA_V_DIM
GLA_GATE_RANK = 16
GLA_TAU = 16.0
GLA_CHUNK = 64
GLA_BLK = 256
D_FF = 2816
EPS = 1e-6

LANES = 128
VMEM_LIMIT = 56 * 1024 * 1024

TM_PROJ = 512
TM_FFN = 512
FFN_CHUNK = 256
DOWN_GROUP = 4
TQ = 256
TK = 256
ROW_BLK = 64


def _dot(a, b):
    return jnp.dot(a, b, preferred_element_type=F32)


def _dot_nt(a, b):
    return lax.dot_general(a, b, (((1,), (1,)), ((), ())), preferred_element_type=F32)


def _dot_tn(a, b):
    return lax.dot_general(a, b, (((0,), (0,)), ((), ())), preferred_element_type=F32)


def _rms(x, w):
    ms = jnp.mean(x * x, axis=-1, keepdims=True)
    return x * lax.rsqrt(ms + EPS) * w


def _const_spec(shape):
    nd = len(shape)
    return pl.BlockSpec(shape, lambda *_: (0,) * nd, pipeline_mode=pl.Buffered(1))


def _in_proj_kernel(x_ref, ln_ref, wa_ref, wg_ref, wr_ref, wg2_ref, bg_ref,
                    a_ref, g_ref, lg_ref):
    h = _rms(x_ref[...], ln_ref[...]).astype(BF16)
    a_ref[...] = _dot(h, wa_ref[...]).astype(BF16)
    g_ref[...] = _dot(h, wg_ref[...]).astype(BF16)
    r = _dot(h, wr_ref[...]).astype(BF16)
    logits = _dot(r, wg2_ref[...]) + bg_ref[...]
    ls = jnp.minimum(logits, 0.0) - jnp.log(1.0 + jnp.exp(-jnp.abs(logits)))
    lg_ref[...] = ls * (1.0 / GLA_TAU)


def _in_proj(x2, ln1_w, wa, wg, wr, wg2, bg):
    n = x2.shape[0]
    tm = TM_PROJ
    return pl.pallas_call(
        _in_proj_kernel,
        grid=(n // tm,),
        in_specs=[
            pl.BlockSpec((tm, D_MODEL), lambda i: (i, 0)),
            _const_spec((1, D_MODEL)),
            _const_spec(wa.shape),
            _const_spec(wg.shape),
            _const_spec(wr.shape),
            _const_spec(wg2.shape),
            _const_spec((1, GLA_KW)),
        ],
        out_specs=[
            pl.BlockSpec((tm, wa.shape[1]), lambda i: (i, 0)),
            pl.BlockSpec((tm, wg.shape[1]), lambda i: (i, 0)),
            pl.BlockSpec((tm, GLA_KW), lambda i: (i, 0)),
        ],
        out_shape=[
            jax.ShapeDtypeStruct((n, wa.shape[1]), BF16),
            jax.ShapeDtypeStruct((n, wg.shape[1]), BF16),
            jax.ShapeDtypeStruct((n, GLA_KW), F32),
        ],
        compiler_params=pltpu.CompilerParams(
            dimension_semantics=("arbitrary",), vmem_limit_bytes=VMEM_LIMIT),
        name="in_proj",
    )(x2, ln1_w, wa, wg, wr, wg2, bg)


def _diff_attn_kernel(lam_init, lq1_ref, lk1_ref, lq2_ref, lk2_ref, sw_ref,
                      q_ref, k_ref, v_ref, o_ref,
                      vt_ref, bias_ref, s_ref4, p_ref4):
    assert TQ == TK
    seq = q_ref.shape[1]
    nq = seq // TQ
    nk = seq // TK

    lam = (jnp.exp(jnp.sum(lq1_ref[...] * lk1_ref[...], axis=-1, keepdims=True))
           - jnp.exp(jnp.sum(lq2_ref[...] * lk2_ref[...], axis=-1, keepdims=True))
           + lam_init)

    for t in range(nk):
        vt_ref[:, t * TK:(t + 1) * TK] = v_ref[0, t * TK:(t + 1) * TK, :].astype(F32).T.astype(BF16)

    row = lax.broadcasted_iota(jnp.int32, (2 * DIFF_QK_DIM, TQ), 0)
    krow = lax.broadcasted_iota(jnp.int32, (TQ, 2 * TQ), 0)
    qcol = lax.broadcasted_iota(jnp.int32, (TQ, 2 * TQ), 1) % TQ
    bias_ref[...] = jnp.where(krow <= qcol, 0.0, -jnp.inf).astype(F32)

    def fold8(x, op):
        parts = [x[a * 8:(a + 1) * 8, :] for a in range(x.shape[0] // 8)]
        while len(parts) > 1:
            parts = [op(parts[a], parts[a + 1]) for a in range(0, len(parts), 2)]
        return parts[0]

    def scores(i):
        nkeys = (i + 1) * TQ
        qt = q_ref[0, i * TQ:(i + 1) * TQ, :].astype(F32).T.astype(BF16)
        zero = jnp.zeros_like(qt)
        q12t = jnp.concatenate([jnp.where(row < DIFF_QK_DIM, qt, zero),
                                jnp.where(row >= DIFF_QK_DIM, qt, zero)], axis=1)
        s_ref4[i % 2, 0:nkeys, :] = _dot(k_ref[0, 0:nkeys, :], q12t)

    def softmax(i):
        nblk = (i + 1) * TQ // ROW_BLK

        def block(r):
            blk = s_ref4[i % 2, r * ROW_BLK:(r + 1) * ROW_BLK, :]
            d = r * ROW_BLK - i * TQ
            if d >= 0:
                blk = blk + bias_ref[d:d + ROW_BLK, :]
            return blk

        m8 = fold8(block(0), jnp.maximum)
        for r in range(1, nblk):
            m8 = jnp.maximum(m8, fold8(block(r), jnp.maximum))
        m = jnp.max(m8, axis=0, keepdims=True)
        mb = jnp.broadcast_to(m, (ROW_BLK, 2 * TQ))
        l8 = jnp.zeros((8, 2 * TQ), F32)
        for r in range(nblk):
            p = jnp.exp(block(r) - mb)
            l8 = l8 + fold8(p, jnp.add)
            p_ref4[i % 2, r * ROW_BLK:(r + 1) * ROW_BLK, :] = p.astype(BF16)
        return jnp.sum(l8, axis=0, keepdims=True)

    def values(i, l):
        nkeys = (i + 1) * TQ
        acc = _dot(vt_ref[:, 0:nkeys], p_ref4[i % 2, 0:nkeys, :]) / l
        ot = acc[:, :TQ] - lam * acc[:, TQ:]
        ms = jnp.mean(ot * ot, axis=0, keepdims=True)
        ot = ot * lax.rsqrt(ms + EPS) * sw_ref[...] * (1.0 - lam_init)
        o_ref[0, i * TQ:(i + 1) * TQ, :] = ot.T.astype(BF16)

    scores(0)
    sums = {}
    for i in range(nq):
        if i > 0:
            values(i - 1, sums.pop(i - 1))
        if i + 1 < nq:
            scores(i + 1)
        sums[i] = softmax(i)
    values(nq - 1, sums.pop(nq - 1))


def _diff_attn(a3, lq1, lk1, lq2, lk2, subln_col, lam_init):
    b, seq, _ = a3.shape
    hd = 2 * DIFF_QK_DIM
    vec = _const_spec((1, DIFF_QK_DIM))
    return pl.pallas_call(
        functools.partial(_diff_attn_kernel, lam_init),
        grid=(b, DIFF_HEADS),
        in_specs=[
            vec, vec, vec, vec,
            _const_spec((DIFF_V_DIM, 1)),
            pl.BlockSpec((1, seq, hd), lambda bi, h: (bi, 0, h)),
            pl.BlockSpec((1, seq, hd), lambda bi, h: (bi, 0, DIFF_HEADS + h)),
            pl.BlockSpec((1, seq, DIFF_V_DIM), lambda bi, h: (bi, 0, 2 * DIFF_HEADS + h)),
        ],
        out_specs=pl.BlockSpec((1, seq, DIFF_V_DIM), lambda bi, h: (bi, 0, h)),
        out_shape=jax.ShapeDtypeStruct((b, seq, DIFF_WIDTH), BF16),
        scratch_shapes=[
            pltpu.VMEM((DIFF_V_DIM, seq), BF16),
            pltpu.VMEM((TQ, 2 * TQ), F32),
            pltpu.VMEM((2, seq, 2 * TQ), F32),
            pltpu.VMEM((2, seq, 2 * TQ), BF16),
        ],
        compiler_params=pltpu.CompilerParams(
            dimension_semantics=("arbitrary", "arbitrary"), vmem_limit_bytes=VMEM_LIMIT),
        name="diff_attn",
    )(lq1, lk1, lq2, lk2, subln_col, a3, a3, a3)


def _gla_kernel(g_ref, lg_ref, nw_ref, o_ref, qd_ref, keh_ref, dec_ref, oi_ref, st_ref):
    c = GLA_CHUNK
    blk = GLA_BLK
    cpb = blk // c
    seq = g_ref.shape[1]
    kw, vw, dv = GLA_KW, GLA_WIDTH, GLA_V_DIM
    pw = 2 * GLA_K_DIM
    assert pw == LANES

    ri = lax.broadcasted_iota(jnp.int32, (blk, blk), 0)
    ci = lax.broadcasted_iota(jnp.int32, (blk, blk), 1)
    intra = (ri // c == ci // c) & (ri >= ci)
    tri = intra.astype(BF16)
    lane_half = lax.broadcasted_iota(jnp.int32, (blk, pw), 1) // GLA_K_DIM
    row_half = lax.broadcasted_iota(jnp.int32, (pw, blk), 0) // GLA_K_DIM

    def pair(h):
        p = h // 2
        return slice(p * pw, (p + 1) * pw)

    def vcols(h):
        return slice(h * dv, (h + 1) * dv)

    st_ref[...] = jnp.zeros_like(st_ref)

    def prep(bi):
        rows = slice(bi * blk, (bi + 1) * blk)
        lg = lg_ref[0, rows, :]
        hi = lg.astype(BF16)
        lo = (lg - hi.astype(F32)).astype(BF16)
        b = _dot(tri, hi) + _dot(tri, lo)
        tot = jnp.concatenate(
            [jnp.broadcast_to(b[j * c + c - 1:(j + 1) * c, :], (c, kw)) for j in range(cpb)], axis=0)
        g = g_ref[0, rows, :]
        q = g[:, 0:kw].astype(F32)
        k = g[:, kw:2 * kw].astype(F32)
        qd = (q * jnp.exp(b)).astype(BF16)
        kd = k * jnp.exp(-b)
        ke = (k * jnp.exp(tot - b)).astype(BF16)
        qd_ref[rows, :] = qd
        dec = jnp.exp(tot)
        for j in range(cpb):
            dec_ref[bi * cpb + j] = dec[j * c:j * c + 8, :]
        kdt = kd.T.astype(BF16)
        for h in range(GLA_HEADS):
            keh_ref[h, rows, :] = jnp.where(lane_half == h % 2, ke[:, pair(h)], jnp.zeros((), BF16))
            kdt_h = jnp.where(row_half == h % 2, kdt[pair(h), :], jnp.zeros((), BF16))
            a = jnp.where(intra, _dot(qd[:, pair(h)], kdt_h), 0.0).astype(BF16)
            oi_ref[rows, vcols(h)] = _dot(a, g[:, 2 * kw + h * dv:2 * kw + (h + 1) * dv])

    def scan(n):
        rows = slice(n * c, (n + 1) * c)
        for h in range(GLA_HEADS):
            st = st_ref[h]
            oi_ref[rows, vcols(h)] += _dot_nt(qd_ref[rows, pair(h)], st.astype(BF16))
            v_h = g_ref[0, rows, 2 * kw + h * dv:2 * kw + (h + 1) * dv]
            st_ref[h] = st * dec_ref[n, 0:1, pair(h)] + _dot_tn(v_h, keh_ref[h, rows, :])

    def finish(bi):
        rows = slice(bi * blk, (bi + 1) * blk)
        nw = nw_ref[...]
        for h in range(GLA_HEADS):
            gate = g_ref[0, rows, 2 * kw + vw + h * dv:2 * kw + vw + (h + 1) * dv].astype(F32)
            y = _rms(oi_ref[rows, vcols(h)], nw) * (gate / (1.0 + jnp.exp(-gate)))
            o_ref[0, rows, vcols(h)] = y.astype(BF16)

    nblk = seq // blk
    prep(0)
    for bi in range(nblk):
        if bi + 1 < nblk:
            prep(bi + 1)
        for n in range(bi * cpb, (bi + 1) * cpb):
            scan(n)
        finish(bi)


def _gla(g3, lg3, norm_w):
    b, seq, gw = g3.shape
    return pl.pallas_call(
        _gla_kernel,
        grid=(b,),
        in_specs=[
            pl.BlockSpec((1, seq, gw), lambda bi: (bi, 0, 0)),
            pl.BlockSpec((1, seq, GLA_KW), lambda bi: (bi, 0, 0)),
            _const_spec((1, GLA_V_DIM)),
        ],
        out_specs=pl.BlockSpec((1, seq, GLA_WIDTH), lambda bi: (bi, 0, 0)),
        out_shape=jax.ShapeDtypeStruct((b, seq, GLA_WIDTH), BF16),
        scratch_shapes=[
            pltpu.VMEM((seq, GLA_KW), BF16),
            pltpu.VMEM((GLA_HEADS, seq, 2 * GLA_K_DIM), BF16),
            pltpu.VMEM((seq // GLA_CHUNK, 8, GLA_KW), F32),
            pltpu.VMEM((seq, GLA_WIDTH), F32),
            pltpu.VMEM((GLA_HEADS, GLA_V_DIM, 2 * GLA_K_DIM), F32),
        ],
        compiler_params=pltpu.CompilerParams(
            dimension_semantics=("arbitrary",), vmem_limit_bytes=VMEM_LIMIT),
        name="gla",
    )(g3, lg3, norm_w)


def _mlp_kernel(tiles_per_seq, x_ref, od_ref, og_ref, wo_ref, ln2_ref, wup_ref, cw_ref, cb_ref,
                wdn_ref, lnf_ref, o_ref, x1_ref, h_ref, u_ref, act_ref, carry_ref):
    tm = x_ref.shape[0]
    halo = 8
    ck = FFN_CHUNK
    nch = D_FF // ck

    @pl.when(pl.program_id(0) % tiles_per_seq == 0)
    def _():
        carry_ref[...] = jnp.zeros_like(carry_ref)

    mix = jnp.concatenate([od_ref[...], og_ref[...]], axis=1)
    x1 = x_ref[...] + _dot(mix, wo_ref[...])
    x1_ref[...] = x1
    h_ref[...] = _rms(x1, ln2_ref[...]).astype(BF16)

    def up(c):
        par = c % 2
        u = _dot(h_ref[...], wup_ref[c])
        u_ref[par, 0:halo, :] = carry_ref[c]
        u_ref[par, halo:halo + tm, :] = u
        carry_ref[c] = u[tm - halo:tm, :]

    def conv_act(c):
        par = c % 2
        w = cw_ref[c]
        y = (u_ref[par, halo:halo + tm, :] * w[2:3, :]
             + u_ref[par, halo - 1:halo - 1 + tm, :] * w[1:2, :]
             + u_ref[par, halo - 2:halo - 2 + tm, :] * w[0:1, :]
             + cb_ref[c])
        gate, val = y[:, :ck], y[:, ck:]
        act_ref[:, c * ck:(c + 1) * ck] = (gate / (1.0 + jnp.exp(-gate)) * val).astype(BF16)

    def down(c0, c1):
        return _dot(act_ref[:, c0 * ck:c1 * ck], wdn_ref[c0 * ck:c1 * ck, :])

    groups = [(g, min(g + DOWN_GROUP, nch)) for g in range(0, nch, DOWN_GROUP)]
    up(0)
    for c in range(nch):
        if c + 1 < nch:
            up(c + 1)
        for g0, g1 in groups[:-1]:
            if g1 == c:
                x1_ref[...] += down(g0, g1)
        conv_act(c)
    o_ref[...] = _rms(x1_ref[...] + down(*groups[-1]), lnf_ref[...])


def _mlp(x2, od, og, wo, ln2_w, wup, cw, cb, wdn, lnf_w, seq):
    n = x2.shape[0]
    tm = TM_FFN
    nch = D_FF // FFN_CHUNK
    tok = lambda w: pl.BlockSpec((tm, w), lambda i: (i, 0))
    return pl.pallas_call(
        functools.partial(_mlp_kernel, seq // tm),
        grid=(n // tm,),
        in_specs=[tok(D_MODEL), tok(DIFF_WIDTH), tok(GLA_WIDTH),
                  _const_spec(wo.shape), _const_spec((1, D_MODEL)),
                  _const_spec(wup.shape), _const_spec(cw.shape), _const_spec(cb.shape),
                  _const_spec(wdn.shape), _const_spec((1, D_MODEL))],
        out_specs=tok(D_MODEL),
        out_shape=jax.ShapeDtypeStruct((n, D_MODEL), F32),
        scratch_shapes=[
            pltpu.VMEM((tm, D_MODEL), F32),
            pltpu.VMEM((tm, D_MODEL), BF16),
            pltpu.VMEM((2, tm + 8, 2 * FFN_CHUNK), F32),
            pltpu.VMEM((tm, D_FF), BF16),
            pltpu.VMEM((nch, 8, 2 * FFN_CHUNK), F32),
        ],
        compiler_params=pltpu.CompilerParams(
            dimension_semantics=("arbitrary",), vmem_limit_bytes=VMEM_LIMIT),
        name="mlp",
    )(x2, od, og, wo, ln2_w, wup, cw, cb, wdn, lnf_w)


def kernel(x, ln1_w, w_in, diff_lq1, diff_lk1, diff_lq2, diff_lk2, diff_subln_w,
           gla_wg2, gla_bg, gla_norm_w, w_out, ln2_w, w_up, conv_w, conv_b,
           w_down, lnf_w):
    b, seq, d = x.shape
    n = b * seq
    depth = w_in.shape[0]
    assert depth == 1, depth
    x2 = x.reshape(n, d)
    qk_w = DIFF_HEADS * 2 * DIFF_QK_DIM
    c0 = 2 * qk_w + DIFF_WIDTH
    c1 = c0 + 2 * GLA_KW + GLA_WIDTH
    c2 = c1 + GLA_GATE_RANK
    for l in range(depth):
        lam_init = 0.8 - 0.6 * math.exp(-0.3 * l)
        w = w_in[l]
        wa = jnp.concatenate([w[:, :qk_w] * DIFF_QK_DIM ** -0.5, w[:, qk_w:c0]], axis=1).astype(BF16)
        wg = jnp.concatenate([w[:, c0:c0 + GLA_KW] * GLA_K_DIM ** -0.5,
                              w[:, c0 + GLA_KW:c1], w[:, c2:]], axis=1).astype(BF16)
        wr = jnp.pad(w[:, c1:c2], ((0, 0), (0, LANES - GLA_GATE_RANK))).astype(BF16)
        wg2 = jnp.pad(gla_wg2[l], ((0, LANES - GLA_GATE_RANK), (0, 0))).astype(BF16)

        a, g, lg = _in_proj(x2, ln1_w[l][None, :], wa, wg, wr, wg2, gla_bg[l][None, :])
        o_diff = _diff_attn(a.reshape(b, seq, -1),
                            diff_lq1[l][None, :], diff_lk1[l][None, :],
                            diff_lq2[l][None, :], diff_lk2[l][None, :],
                            diff_subln_w[l][:, None], lam_init)
        o_gla = _gla(g.reshape(b, seq, -1), lg.reshape(b, seq, -1), gla_norm_w[l][None, :])

        nch = D_FF // FFN_CHUNK

        def by_chunk(t):
            r = t.shape[0]
            return t.reshape(r, 2, nch, FFN_CHUNK).transpose(2, 0, 1, 3).reshape(nch, r, 2 * FFN_CHUNK)

        x2 = _mlp(x2, o_diff.reshape(n, -1), o_gla.reshape(n, -1), w_out[l].astype(BF16),
                  ln2_w[l][None, :], by_chunk(w_up[l].astype(BF16)), by_chunk(conv_w[l]),
                  by_chunk(conv_b[l][None, :]), w_down[l].astype(BF16), lnf_w[None, :], seq)
    return x2.reshape(b, seq, d)
```

```python
import functools
import math

import jax
import jax.numpy as jnp
from jax import lax
from jax.experimental import pallas as pl
from jax.experimental.pallas import tpu as pltpu

F32 = jnp.float32
BF16 = jnp.bfloat16

D_MODEL = 1024
DIFF_HEADS = 4
DIFF_QK_DIM = 64
DIFF_V_DIM = 128
DIFF_WIDTH = DIFF_HEADS * DIFF_V_DIM
GLA_HEADS = 4
GLA_K_DIM = 64
GLA_V_DIM = 128
GLA_KW = GLA_HEADS * GLA_K_DIM
GLA_WIDTH = GLA_HEADS * GLA_V_DIM
GLA_GATE_RANK = 16
GLA_TAU = 16.0
GLA_CHUNK = 64
GLA_BLK = 256
D_FF = 2816
EPS = 1e-6

LANES = 128
VMEM_LIMIT = 56 * 1024 * 1024

TM_PROJ = 512
TM_FFN = 512
FFN_CHUNK = 256
DOWN_GROUP = 4
TQ = 256
TK = 256
ROW_BLK = 64


def _dot(a, b):
    return jnp.dot(a, b, preferred_element_type=F32)


def _dot_nt(a, b):
    return lax.dot_general(a, b, (((1,), (1,)), ((), ())), preferred_element_type=F32)


def _dot_tn(a, b):
    return lax.dot_general(a, b, (((0,), (0,)), ((), ())), preferred_element_type=F32)


def _rms(x, w):
    ms = jnp.mean(x * x, axis=-1, keepdims=True)
    return x * lax.rsqrt(ms + EPS) * w


def _const_spec(shape):
    nd = len(shape)
    return pl.BlockSpec(shape, lambda *_: (0,) * nd, pipeline_mode=pl.Buffered(1))


def _in_proj_kernel(x_ref, ln_ref, wa_ref, wg_ref, wr_ref, wg2_ref, bg_ref,
                    a_ref, g_ref, lg_ref):
    h = _rms(x_ref[...], ln_ref[...]).astype(BF16)
    a_ref[...] = _dot(h, wa_ref[...]).astype(BF16)
    g_ref[...] = _dot(h, wg_ref[...]).astype(BF16)
    r = _dot(h, wr_ref[...]).astype(BF16)
    logits = _dot(r, wg2_ref[...]) + bg_ref[...]
    ls = jnp.minimum(logits, 0.0) - jnp.log(1.0 + jnp.exp(-jnp.abs(logits)))
    lg_ref[...] = ls * (1.0 / GLA_TAU)


def _in_proj(x2, ln1_w, wa, wg, wr, wg2, bg):
    n = x2.shape[0]
    tm = TM_PROJ
    return pl.pallas_call(
        _in_proj_kernel,
        grid=(n // tm,),
        in_specs=[
            pl.BlockSpec((tm, D_MODEL), lambda i: (i, 0)),
            _const_spec((1, D_MODEL)),
            _const_spec(wa.shape),
            _const_spec(wg.shape),
            _const_spec(wr.shape),
            _const_spec(wg2.shape),
            _const_spec((1, GLA_KW)),
        ],
        out_specs=[
            pl.BlockSpec((tm, wa.shape[1]), lambda i: (i, 0)),
            pl.BlockSpec((tm, wg.shape[1]), lambda i: (i, 0)),
            pl.BlockSpec((tm, GLA_KW), lambda i: (i, 0)),
        ],
        out_shape=[
            jax.ShapeDtypeStruct((n, wa.shape[1]), BF16),
            jax.ShapeDtypeStruct((n, wg.shape[1]), BF16),
            jax.ShapeDtypeStruct((n, GLA_KW), F32),
        ],
        compiler_params=pltpu.CompilerParams(
            dimension_semantics=("arbitrary",), vmem_limit_bytes=VMEM_LIMIT),
        name="in_proj",
    )(x2, ln1_w, wa, wg, wr, wg2, bg)


def _diff_attn_kernel(lam_init, lq1_ref, lk1_ref, lq2_ref, lk2_ref, sw_ref,
                      q_ref, k_ref, v_ref, o_ref,
                      vt_ref, bias_ref, s_ref4, p_ref4):
    assert TQ == TK
    seq = q_ref.shape[1]
    nq = seq // TQ
    nk = seq // TK

    lam = (jnp.exp(jnp.sum(lq1_ref[...] * lk1_ref[...], axis=-1, keepdims=True))
           - jnp.exp(jnp.sum(lq2_ref[...] * lk2_ref[...], axis=-1, keepdims=True))
           + lam_init)

    for t in range(nk):
        vt_ref[:, t * TK:(t + 1) * TK] = v_ref[0, t * TK:(t + 1) * TK, :].astype(F32).T.astype(BF16)

    row = lax.broadcasted_iota(jnp.int32, (2 * DIFF_QK_DIM, TQ), 0)
    krow = lax.broadcasted_iota(jnp.int32, (TQ, 2 * TQ), 0)
    qcol = lax.broadcasted_iota(jnp.int32, (TQ, 2 * TQ), 1) % TQ
    bias_ref[...] = jnp.where(krow <= qcol, 0.0, -jnp.inf).astype(F32)

    def fold8(x, op):
        parts = [x[a * 8:(a + 1) * 8, :] for a in range(x.shape[0] // 8)]
        while len(parts) > 1:
            parts = [op(parts[a], parts[a + 1]) for a in range(0, len(parts), 2)]
        return parts[0]

    def scores(i):
        nkeys = (i + 1) * TQ
        qt = q_ref[0, i * TQ:(i + 1) * TQ, :].astype(F32).T.astype(BF16)
        zero = jnp.zeros_like(qt)
        q12t = jnp.concatenate([jnp.where(row < DIFF_QK_DIM, qt, zero),
                                jnp.where(row >= DIFF_QK_DIM, qt, zero)], axis=1)
        s_ref4[i % 2, 0:nkeys, :] = _dot(k_ref[0, 0:nkeys, :], q12t)

    def softmax(i):
        nblk = (i + 1) * TQ // ROW_BLK

        def block(r):
            blk = s_ref4[i % 2, r * ROW_BLK:(r + 1) * ROW_BLK, :]
            d = r * ROW_BLK - i * TQ
            if d >= 0:
                blk = blk + bias_ref[d:d + ROW_BLK, :]
            return blk

        m8 = fold8(block(0), jnp.maximum)
        for r in range(1, nblk):
            m8 = jnp.maximum(m8, fold8(block(r), jnp.maximum))
        m = jnp.max(m8, axis=0, keepdims=True)
        mb = jnp.broadcast_to(m, (ROW_BLK, 2 * TQ))
        l8 = jnp.zeros((8, 2 * TQ), F32)
        for r in range(nblk):
            p = jnp.exp(block(r) - mb)
            l8 = l8 + fold8(p, jnp.add)
            p_ref4[i % 2, r * ROW_BLK:(r + 1) * ROW_BLK, :] = p.astype(BF16)
        return jnp.sum(l8, axis=0, keepdims=True)

    def values(i, l):
        nkeys = (i + 1) * TQ
        acc = _dot(vt_ref[:, 0:nkeys], p_ref4[i % 2, 0:nkeys, :]) / l
        ot = acc[:, :TQ] - lam * acc[:, TQ:]
        ms = jnp.mean(ot * ot, axis=0, keepdims=True)
        ot = ot * lax.rsqrt(ms + EPS) * sw_ref[...] * (1.0 - lam_init)
        o_ref[0, i * TQ:(i + 1) * TQ, :] = ot.T.astype(BF16)

    scores(0)
    sums = {}
    for i in range(nq):
        if i > 0:
            values(i - 1, sums.pop(i - 1))
        if i + 1 < nq:
            scores(i + 1)
        sums[i] = softmax(i)
    values(nq - 1, sums.pop(nq - 1))


def _diff_attn(a3, lq1, lk1, lq2, lk2, subln_col, lam_init):
    b, seq, _ = a3.shape
    hd = 2 * DIFF_QK_DIM
    vec = _const_spec((1, DIFF_QK_DIM))
    return pl.pallas_call(
        functools.partial(_diff_attn_kernel, lam_init),
        grid=(b, DIFF_HEADS),
        in_specs=[
            vec, vec, vec, vec,
            _const_spec((DIFF_V_DIM, 1)),
            pl.BlockSpec((1, seq, hd), lambda bi, h: (bi, 0, h)),
            pl.BlockSpec((1, seq, hd), lambda bi, h: (bi, 0, DIFF_HEADS + h)),
            pl.BlockSpec((1, seq, DIFF_V_DIM), lambda bi, h: (bi, 0, 2 * DIFF_HEADS + h)),
        ],
        out_specs=pl.BlockSpec((1, seq, DIFF_V_DIM), lambda bi, h: (bi, 0, h)),
        out_shape=jax.ShapeDtypeStruct((b, seq, DIFF_WIDTH), BF16),
        scratch_shapes=[
            pltpu.VMEM((DIFF_V_DIM, seq), BF16),
            pltpu.VMEM((TQ, 2 * TQ), F32),
            pltpu.VMEM((2, seq, 2 * TQ), F32),
            pltpu.VMEM((2, seq, 2 * TQ), BF16),
        ],
        compiler_params=pltpu.CompilerParams(
            dimension_semantics=("arbitrary", "arbitrary"), vmem_limit_bytes=VMEM_LIMIT),
        name="diff_attn",
    )(lq1, lk1, lq2, lk2, subln_col, a3, a3, a3)


def _gla_kernel(g_ref, lg_ref, nw_ref, o_ref, qd_ref, keh_ref, dec_ref, oi_ref, st_ref):
    c = GLA_CHUNK
    blk = GLA_BLK
    cpb = blk // c
    seq = g_ref.shape[1]
    kw, vw, dv = GLA_KW, GLA_WIDTH, GLA_V_DIM
    pw = 2 * GLA_K_DIM
    assert pw == LANES

    ri = lax.broadcasted_iota(jnp.int32, (blk, blk), 0)
    ci = lax.broadcasted_iota(jnp.int32, (blk, blk), 1)
    intra = (ri // c == ci // c) & (ri >= ci)
    tri = intra.astype(BF16)
    lane_half = lax.broadcasted_iota(jnp.int32, (blk, pw), 1) // GLA_K_DIM
    row_half = lax.broadcasted_iota(jnp.int32, (pw, blk), 0) // GLA_K_DIM

    def pair(h):
        p = h // 2
        return slice(p * pw, (p + 1) * pw)

    def vcols(h):
        return slice(h * dv, (h + 1) * dv)

    st_ref[...] = jnp.zeros_like(st_ref)

    def span(start, size):
        if isinstance(start, int):
            return slice(start, start + size)
        return pl.ds(pl.multiple_of(start, size), size)

    def prep(bi):
        rows = span(bi * blk, blk)
        lg = lg_ref[0, rows, :]
        hi = lg.astype(BF16)
        lo = (lg - hi.astype(F32)).astype(BF16)
        b = _dot(tri, hi) + _dot(tri, lo)
        tot = jnp.concatenate(
            [jnp.broadcast_to(b[j * c + c - 1:(j + 1) * c, :], (c, kw)) for j in range(cpb)], axis=0)
        g = g_ref[0, rows, :]
        q = g[:, 0:kw].astype(F32)
        k = g[:, kw:2 * kw].astype(F32)
        qd = (q * jnp.exp(b)).astype(BF16)
        kd = k * jnp.exp(-b)
        ke = (k * jnp.exp(tot - b)).astype(BF16)
        qd_ref[rows, :] = qd
        dec = jnp.exp(tot)
        for j in range(cpb):
            dec_ref[bi * cpb + j] = dec[j * c:j * c + 8, :]
        kdt = kd.T.astype(BF16)
        for h in range(GLA_HEADS):
            keh_ref[h, rows, :] = jnp.where(lane_half == h % 2, ke[:, pair(h)], jnp.zeros((), BF16))
            kdt_h = jnp.where(row_half == h % 2, kdt[pair(h), :], jnp.zeros((), BF16))
            a = jnp.where(intra, _dot(qd[:, pair(h)], kdt_h), 0.0).astype(BF16)
            oi_ref[rows, vcols(h)] = _dot(a, g[:, 2 * kw + h * dv:2 * kw + (h + 1) * dv])

    def scan(n):
        rows = span(n * c, c)
        for h in range(GLA_HEADS):
            st = st_ref[h]
            oi_ref[rows, vcols(h)] += _dot_nt(qd_ref[rows, pair(h)], st.astype(BF16))
            v_h = g_ref[0, rows, 2 * kw + h * dv:2 * kw + (h + 1) * dv]
            st_ref[h] = st * dec_ref[n, 0:1, pair(h)] + _dot_tn(v_h, keh_ref[h, rows, :])

    def scan_block(bi):
        for j in range(cpb):
            scan(bi * cpb + j)

    def finish(bi):
        rows = span(bi * blk, blk)
        nw = nw_ref[...]
        for h in range(GLA_HEADS):
            gate = g_ref[0, rows, 2 * kw + vw + h * dv:2 * kw + vw + (h + 1) * dv].astype(F32)
            y = _rms(oi_ref[rows, vcols(h)], nw) * (gate / (1.0 + jnp.exp(-gate)))
            o_ref[0, rows, vcols(h)] = y.astype(BF16)

    nblk = seq // blk
    prep(0)
    for bi in range(nblk):
        if bi + 1 < nblk:
            prep(bi + 1)
        scan_block(bi)
        finish(bi)


def _gla(g3, lg3, norm_w):
    b, seq, gw = g3.shape
    return pl.pallas_call(
        _gla_kernel,
        grid=(b,),
        in_specs=[
            pl.BlockSpec((1, seq, gw), lambda bi: (bi, 0, 0)),
            pl.BlockSpec((1, seq, GLA_KW), lambda bi: (bi, 0, 0)),
            _const_spec((1, GLA_V_DIM)),
        ],
        out_specs=pl.BlockSpec((1, seq, GLA_WIDTH), lambda bi: (bi, 0, 0)),
        out_shape=jax.ShapeDtypeStruct((b, seq, GLA_WIDTH), BF16),
        scratch_shapes=[
            pltpu.VMEM((seq, GLA_KW), BF16),
            pltpu.VMEM((GLA_HEADS, seq, 2 * GLA_K_DIM), BF16),
            pltpu.VMEM((seq // GLA_CHUNK, 8, GLA_KW), F32),
            pltpu.VMEM((seq, GLA_WIDTH), F32),
            pltpu.VMEM((GLA_HEADS, GLA_V_DIM, 2 * GLA_K_DIM), F32),
        ],
        compiler_params=pltpu.CompilerParams(
            dimension_semantics=("arbitrary",), vmem_limit_bytes=VMEM_LIMIT),
        name="gla",
    )(g3, lg3, norm_w)


def _mlp_kernel(tiles_per_seq, x_ref, od_ref, og_ref, wo_ref, ln2_ref, wup_ref, cw_ref, cb_ref,
                wdn_ref, lnf_ref, o_ref, x1_ref, h_ref, u_ref, act_ref, carry_ref):
    tm = x_ref.shape[0]
    halo = 8
    ck = FFN_CHUNK
    nch = D_FF // ck

    @pl.when(pl.program_id(0) % tiles_per_seq == 0)
    def _():
        carry_ref[...] = jnp.zeros_like(carry_ref)

    mix = jnp.concatenate([od_ref[...], og_ref[...]], axis=1)
    x1 = x_ref[...] + _dot(mix, wo_ref[...])
    x1_ref[...] = x1
    h_ref[...] = _rms(x1, ln2_ref[...]).astype(BF16)

    def gate_value(ref, c):
        return jnp.concatenate([ref[:, c * ck:(c + 1) * ck],
                                ref[:, D_FF + c * ck:D_FF + (c + 1) * ck]], axis=1)

    def up(c):
        par = c % 2
        u = _dot(h_ref[...], gate_value(wup_ref, c))
        u_ref[par, 0:halo, :] = carry_ref[c]
        u_ref[par, halo:halo + tm, :] = u
        carry_ref[c] = u[tm - halo:tm, :]

    def conv_act(c):
        par = c % 2
        w = gate_value(cw_ref, c)
        y = (u_ref[par, halo:halo + tm, :] * w[2:3, :]
             + u_ref[par, halo - 1:halo - 1 + tm, :] * w[1:2, :]
             + u_ref[par, halo - 2:halo - 2 + tm, :] * w[0:1, :]
             + gate_value(cb_ref, c))
        gate, val = y[:, :ck], y[:, ck:]
        act_ref[:, c * ck:(c + 1) * ck] = (gate / (1.0 + jnp.exp(-gate)) * val).astype(BF16)

    def down(c0, c1):
        return _dot(act_ref[:, c0 * ck:c1 * ck], wdn_ref[c0 * ck:c1 * ck, :])

    groups = [(g, min(g + DOWN_GROUP, nch)) for g in range(0, nch, DOWN_GROUP)]
    up(0)
    for c in range(nch):
        if c + 1 < nch:
            up(c + 1)
        for g0, g1 in groups[:-1]:
            if g1 == c:
                x1_ref[...] += down(g0, g1)
        conv_act(c)
    o_ref[...] = _rms(x1_ref[...] + down(*groups[-1]), lnf_ref[...])


def _mlp(x2, od, og, wo, ln2_w, wup, cw, cb, wdn, lnf_w, seq):
    n = x2.shape[0]
    tm = TM_FFN
    nch = D_FF // FFN_CHUNK
    tok = lambda w: pl.BlockSpec((tm, w), lambda i: (i, 0))
    return pl.pallas_call(
        functools.partial(_mlp_kernel, seq // tm),
        grid=(n // tm,),
        in_specs=[tok(D_MODEL), tok(DIFF_WIDTH), tok(GLA_WIDTH),
                  _const_spec(wo.shape), _const_spec((1, D_MODEL)),
                  _const_spec(wup.shape), _const_spec(cw.shape), _const_spec(cb.shape),
                  _const_spec(wdn.shape), _const_spec((1, D_MODEL))],
        out_specs=tok(D_MODEL),
        out_shape=jax.ShapeDtypeStruct((n, D_MODEL), F32),
        scratch_shapes=[
            pltpu.VMEM((tm, D_MODEL), F32),
            pltpu.VMEM((tm, D_MODEL), BF16),
            pltpu.VMEM((2, tm + 8, 2 * FFN_CHUNK), F32),
            pltpu.VMEM((tm, D_FF), BF16),
            pltpu.VMEM((nch, 8, 2 * FFN_CHUNK), F32),
        ],
        compiler_params=pltpu.CompilerParams(
            dimension_semantics=("arbitrary",), vmem_limit_bytes=VMEM_LIMIT),
        name="mlp",
    )(x2, od, og, wo, ln2_w, wup, cw, cb, wdn, lnf_w)


def kernel(x, ln1_w, w_in, diff_lq1, diff_lk1, diff_lq2, diff_lk2, diff_subln_w,
           gla_wg2, gla_bg, gla_norm_w, w_out, ln2_w, w_up, conv_w, conv_b,
           w_down, lnf_w):
    b, seq, d = x.shape
    n = b * seq
    depth = w_in.shape[0]
    assert depth == 1, depth
    x2 = x.reshape(n, d)
    qk_w = DIFF_HEADS * 2 * DIFF_QK_DIM
    c0 = 2 * qk_w + DIFF_WIDTH
    c1 = c0 + 2 * GLA_KW + GLA_WIDTH
    c2 = c1 + GLA_GATE_RANK
    for l in range(depth):
        lam_init = 0.8 - 0.6 * math.exp(-0.3 * l)
        w = w_in[l]
        wa = jnp.concatenate([w[:, :qk_w] * DIFF_QK_DIM ** -0.5, w[:, qk_w:c0]], axis=1).astype(BF16)
        wg = jnp.concatenate([w[:, c0:c0 + GLA_KW] * GLA_K_DIM ** -0.5,
                              w[:, c0 + GLA_KW:c1], w[:, c2:]], axis=1).astype(BF16)
        wr = jnp.pad(w[:, c1:c2], ((0, 0), (0, LANES - GLA_GATE_RANK))).astype(BF16)
        wg2 = jnp.pad(gla_wg2[l], ((0, LANES - GLA_GATE_RANK), (0, 0))).astype(BF16)

        a, g, lg = _in_proj(x2, ln1_w[l][None, :], wa, wg, wr, wg2, gla_bg[l][None, :])
        o_diff = _diff_attn(a.reshape(b, seq, -1),
                            diff_lq1[l][None, :], diff_lk1[l][None, :],
                            diff_lq2[l][None, :], diff_lk2[l][None, :],
                            diff_subln_w[l][:, None], lam_init)
        o_gla = _gla(g.reshape(b, seq, -1), lg.reshape(b, seq, -1), gla_norm_w[l][None, :])

        x2 = _mlp(x2, o_diff.reshape(n, -1), o_gla.reshape(n, -1), w_out[l].astype(BF16),
                  ln2_w[l][None, :], w_up[l].astype(BF16), conv_w[l], conv_b[l][None, :],
                  w_down[l].astype(BF16), lnf_w[None, :], seq)
    return x2.reshape(b, seq, d)
```

```python
import functools
import math

import jax
import jax.numpy as jnp
from jax import lax
from jax.experimental import pallas as pl
from jax.experimental.pallas import tpu as pltpu

F32 = jnp.float32
BF16 = jnp.bfloat16

D_MODEL = 1024
DIFF_HEADS = 4
DIFF_QK_DIM = 64
DIFF_V_DIM = 128
DIFF_WIDTH = DIFF_HEADS * DIFF_V_DIM
GLA_HEADS = 4
GLA_K_DIM = 64
GLA_V_DIM = 128
GLA_KW = GLA_HEADS * GLA_K_DIM
GLA_WIDTH = GLA_HEADS * GLA_V_DIM
GLA_GATE_RANK = 16
GLA_TAU = 16.0
GLA_CHUNK = 64
GLA_BLK = 256
D_FF = 2816
EPS = 1e-6

LANES = 128
VMEM_LIMIT = 56 * 1024 * 1024

TM_PROJ = 512
TM_FFN = 512
FFN_CHUNK = 256
DOWN_GROUP = 4
TQ = 256
TK = 256
ROW_BLK = 64


def _dot(a, b):
    return jnp.dot(a, b, preferred_element_type=F32)


def _dot_nt(a, b):
    return lax.dot_general(a, b, (((1,), (1,)), ((), ())), preferred_element_type=F32)


def _dot_tn(a, b):
    return lax.dot_general(a, b, (((0,), (0,)), ((), ())), preferred_element_type=F32)


def _rms(x, w):
    ms = jnp.mean(x * x, axis=-1, keepdims=True)
    return x * lax.rsqrt(ms + EPS) * w


def _const_spec(shape):
    nd = len(shape)
    return pl.BlockSpec(shape, lambda *_: (0,) * nd, pipeline_mode=pl.Buffered(1))


IN_A = 2 * DIFF_HEADS * 2 * DIFF_QK_DIM + DIFF_WIDTH
IN_G = 2 * GLA_KW + 2 * GLA_WIDTH
IN_R = LANES


def _in_proj_kernel(x_ref, ln_ref, w_ref, wg2_ref, bg_ref, wup_ref, wdn_ref, wo_ref,
                    a_ref, g_ref, lg_ref, wup16_ref, wdn16_ref, wo16_ref):
    h = _rms(x_ref[...], ln_ref[...]).astype(BF16)
    a_ref[...] = _dot(h, w_ref[:, 0:IN_A]).astype(BF16)
    g_ref[...] = _dot(h, w_ref[:, IN_A:IN_A + IN_G]).astype(BF16)
    r = _dot(h, w_ref[:, IN_A + IN_G:IN_A + IN_G + IN_R]).astype(BF16)
    logits = _dot(r, wg2_ref[...]) + bg_ref[...]
    ls = jnp.minimum(logits, 0.0) - jnp.log(1.0 + jnp.exp(-jnp.abs(logits)))
    lg_ref[...] = ls * (1.0 / GLA_TAU)
    wup16_ref[...] = wup_ref[...].astype(BF16)
    wdn16_ref[...] = wdn_ref[...].astype(BF16)
    wo16_ref[...] = wo_ref[...].astype(BF16)


def _in_proj(x2, ln1_w, w_all, wg2, bg, w_up, w_down, w_out):
    n = x2.shape[0]
    tm = TM_PROJ
    steps = n // tm
    up_rows = w_up.shape[0] // steps
    out_rows = w_out.shape[0] // steps
    dn_rows = 2 * w_down.shape[0] // steps
    assert up_rows * steps == w_up.shape[0] and out_rows * steps == w_out.shape[0]
    assert dn_rows * steps == 2 * w_down.shape[0]
    assert up_rows % 16 == 0 and out_rows % 16 == 0 and dn_rows % 16 == 0
    slab = lambda rows, w, every: pl.BlockSpec((rows, w), lambda i: (i // every, 0))
    w_specs = [slab(up_rows, w_up.shape[1], 1), slab(dn_rows, w_down.shape[1], 2),
               slab(out_rows, w_out.shape[1], 1)]
    return pl.pallas_call(
        _in_proj_kernel,
        grid=(steps,),
        in_specs=[
            pl.BlockSpec((tm, D_MODEL), lambda i: (i, 0)),
            _const_spec((1, D_MODEL)),
            _const_spec(w_all.shape),
            _const_spec(wg2.shape),
            _const_spec((1, GLA_KW)),
        ] + w_specs,
        out_specs=[
            pl.BlockSpec((tm, IN_A), lambda i: (i, 0)),
            pl.BlockSpec((tm, IN_G), lambda i: (i, 0)),
            pl.BlockSpec((tm, GLA_KW), lambda i: (i, 0)),
        ] + w_specs,
        out_shape=[
            jax.ShapeDtypeStruct((n, IN_A), BF16),
            jax.ShapeDtypeStruct((n, IN_G), BF16),
            jax.ShapeDtypeStruct((n, GLA_KW), F32),
            jax.ShapeDtypeStruct(w_up.shape, BF16),
            jax.ShapeDtypeStruct(w_down.shape, BF16),
            jax.ShapeDtypeStruct(w_out.shape, BF16),
        ],
        compiler_params=pltpu.CompilerParams(
            dimension_semantics=("arbitrary",), vmem_limit_bytes=VMEM_LIMIT),
        name="in_proj",
    )(x2, ln1_w, w_all, wg2, bg, w_up, w_down, w_out)


def _diff_attn_kernel(lam_init, lq1_ref, lk1_ref, lq2_ref, lk2_ref, sw_ref,
                      q_ref, k_ref, v_ref, o_ref,
                      vt_ref, bias_ref, s_ref4, p_ref4):
    assert TQ == TK
    seq = q_ref.shape[1]
    nq = seq // TQ
    nk = seq // TK

    lam = (jnp.exp(jnp.sum(lq1_ref[...] * lk1_ref[...], axis=-1, keepdims=True))
           - jnp.exp(jnp.sum(lq2_ref[...] * lk2_ref[...], axis=-1, keepdims=True))
           + lam_init)

    for t in range(nk):
        vt_ref[:, t * TK:(t + 1) * TK] = v_ref[0, t * TK:(t + 1) * TK, :].astype(F32).T.astype(BF16)

    row = lax.broadcasted_iota(jnp.int32, (2 * DIFF_QK_DIM, TQ), 0)
    krow = lax.broadcasted_iota(jnp.int32, (TQ, 2 * TQ), 0)
    qcol = lax.broadcasted_iota(jnp.int32, (TQ, 2 * TQ), 1) % TQ
    bias_ref[...] = jnp.where(krow <= qcol, 0.0, -jnp.inf).astype(F32)

    def fold8(x, op):
        parts = [x[a * 8:(a + 1) * 8, :] for a in range(x.shape[0] // 8)]
        while len(parts) > 1:
            parts = [op(parts[a], parts[a + 1]) for a in range(0, len(parts), 2)]
        return parts[0]

    def scores(i):
        nkeys = (i + 1) * TQ
        qt = q_ref[0, i * TQ:(i + 1) * TQ, :].astype(F32).T.astype(BF16)
        zero = jnp.zeros_like(qt)
        q12t = jnp.concatenate([jnp.where(row < DIFF_QK_DIM, qt, zero),
                                jnp.where(row >= DIFF_QK_DIM, qt, zero)], axis=1)
        s_ref4[i % 2, 0:nkeys, :] = _dot(k_ref[0, 0:nkeys, :], q12t)

    def softmax(i):
        nblk = (i + 1) * TQ // ROW_BLK

        def block(r):
            blk = s_ref4[i % 2, r * ROW_BLK:(r + 1) * ROW_BLK, :]
            d = r * ROW_BLK - i * TQ
            if d >= 0:
                blk = blk + bias_ref[d:d + ROW_BLK, :]
            return blk

        m8 = fold8(block(0), jnp.maximum)
        for r in range(1, nblk):
            m8 = jnp.maximum(m8, fold8(block(r), jnp.maximum))
        m = jnp.max(m8, axis=0, keepdims=True)
        mb = jnp.broadcast_to(m, (ROW_BLK, 2 * TQ))
        l8 = jnp.zeros((8, 2 * TQ), F32)
        for r in range(nblk):
            p = jnp.exp(block(r) - mb)
            l8 = l8 + fold8(p, jnp.add)
            p_ref4[i % 2, r * ROW_BLK:(r + 1) * ROW_BLK, :] = p.astype(BF16)
        return jnp.sum(l8, axis=0, keepdims=True)

    def values(i, l):
        nkeys = (i + 1) * TQ
        acc = _dot(vt_ref[:, 0:nkeys], p_ref4[i % 2, 0:nkeys, :]) / l
        ot = acc[:, :TQ] - lam * acc[:, TQ:]
        ms = jnp.mean(ot * ot, axis=0, keepdims=True)
        ot = ot * lax.rsqrt(ms + EPS) * sw_ref[...] * (1.0 - lam_init)
        o_ref[0, i * TQ:(i + 1) * TQ, :] = ot.T.astype(BF16)

    scores(0)
    sums = {}
    for i in range(nq):
        if i > 0:
            values(i - 1, sums.pop(i - 1))
        if i + 1 < nq:
            scores(i + 1)
        sums[i] = softmax(i)
    values(nq - 1, sums.pop(nq - 1))


def _diff_attn(a3, lq1, lk1, lq2, lk2, subln_col, lam_init):
    b, seq, _ = a3.shape
    hd = 2 * DIFF_QK_DIM
    vec = _const_spec((1, DIFF_QK_DIM))
    return pl.pallas_call(
        functools.partial(_diff_attn_kernel, lam_init),
        grid=(b, DIFF_HEADS),
        in_specs=[
            vec, vec, vec, vec,
            _const_spec((DIFF_V_DIM, 1)),
            pl.BlockSpec((1, seq, hd), lambda bi, h: (bi, 0, h)),
            pl.BlockSpec((1, seq, hd), lambda bi, h: (bi, 0, DIFF_HEADS + h)),
            pl.BlockSpec((1, seq, DIFF_V_DIM), lambda bi, h: (bi, 0, 2 * DIFF_HEADS + h)),
        ],
        out_specs=pl.BlockSpec((1, seq, DIFF_V_DIM), lambda bi, h: (bi, 0, h)),
        out_shape=jax.ShapeDtypeStruct((b, seq, DIFF_WIDTH), BF16),
        scratch_shapes=[
            pltpu.VMEM((DIFF_V_DIM, seq), BF16),
            pltpu.VMEM((TQ, 2 * TQ), F32),
            pltpu.VMEM((2, seq, 2 * TQ), F32),
            pltpu.VMEM((2, seq, 2 * TQ), BF16),
        ],
        compiler_params=pltpu.CompilerParams(
            dimension_semantics=("arbitrary", "arbitrary"), vmem_limit_bytes=VMEM_LIMIT),
        name="diff_attn",
    )(lq1, lk1, lq2, lk2, subln_col, a3, a3, a3)


def _gla_kernel(g_ref, lg_ref, nw_ref, o_ref, qd_ref, keh_ref, dec_ref, oi_ref, st_ref):
    c = GLA_CHUNK
    blk = GLA_BLK
    cpb = blk // c
    seq = g_ref.shape[1]
    kw, vw, dv = GLA_KW, GLA_WIDTH, GLA_V_DIM
    pw = 2 * GLA_K_DIM
    assert pw == LANES

    ri = lax.broadcasted_iota(jnp.int32, (blk, blk), 0)
    ci = lax.broadcasted_iota(jnp.int32, (blk, blk), 1)
    intra = (ri // c == ci // c) & (ri >= ci)
    tri = intra.astype(BF16)
    lane_half = lax.broadcasted_iota(jnp.int32, (blk, pw), 1) // GLA_K_DIM
    row_half = lax.broadcasted_iota(jnp.int32, (pw, blk), 0) // GLA_K_DIM

    def pair(h):
        p = h // 2
        return slice(p * pw, (p + 1) * pw)

    def vcols(h):
        return slice(h * dv, (h + 1) * dv)

    st_ref[...] = jnp.zeros_like(st_ref)

    def span(start, size):
        if isinstance(start, int):
            return slice(start, start + size)
        return pl.ds(pl.multiple_of(start, size), size)

    def prep(bi):
        rows = span(bi * blk, blk)
        lg = lg_ref[0, rows, :]
        hi = lg.astype(BF16)
        lo = (lg - hi.astype(F32)).astype(BF16)
        b = _dot(tri, hi) + _dot(tri, lo)
        tot = jnp.concatenate(
            [jnp.broadcast_to(b[j * c + c - 1:(j + 1) * c, :], (c, kw)) for j in range(cpb)], axis=0)
        g = g_ref[0, rows, :]
        q = g[:, 0:kw].astype(F32)
        k = g[:, kw:2 * kw].astype(F32)
        qd = (q * jnp.exp(b)).astype(BF16)
        kd = k * jnp.exp(-b)
        ke = (k * jnp.exp(tot - b)).astype(BF16)
        qd_ref[rows, :] = qd
        dec = jnp.exp(tot)
        for j in range(cpb):
            dec_ref[bi * cpb + j] = dec[j * c:j * c + 8, :]
        kdt = kd.T.astype(BF16)
        for h in range(GLA_HEADS):
            keh_ref[h, rows, :] = jnp.where(lane_half == h % 2, ke[:, pair(h)], jnp.zeros((), BF16))
            kdt_h = jnp.where(row_half == h % 2, kdt[pair(h), :], jnp.zeros((), BF16))
            a = jnp.where(intra, _dot(qd[:, pair(h)], kdt_h), 0.0).astype(BF16)
            oi_ref[rows, vcols(h)] = _dot(a, g[:, 2 * kw + h * dv:2 * kw + (h + 1) * dv])

    def scan(n):
        rows = span(n * c, c)
        for h in range(GLA_HEADS):
            st = st_ref[h]
            oi_ref[rows, vcols(h)] += _dot_nt(qd_ref[rows, pair(h)], st.astype(BF16))
            v_h = g_ref[0, rows, 2 * kw + h * dv:2 * kw + (h + 1) * dv]
            st_ref[h] = st * dec_ref[n, 0:1, pair(h)] + _dot_tn(v_h, keh_ref[h, rows, :])

    def scan_block(bi):
        for j in range(cpb):
            scan(bi * cpb + j)

    def finish(bi):
        rows = span(bi * blk, blk)
        nw = nw_ref[...]
        for h in range(GLA_HEADS):
            gate = g_ref[0, rows, 2 * kw + vw + h * dv:2 * kw + vw + (h + 1) * dv].astype(F32)
            y = _rms(oi_ref[rows, vcols(h)], nw) * (gate / (1.0 + jnp.exp(-gate)))
            o_ref[0, rows, vcols(h)] = y.astype(BF16)

    nblk = seq // blk
    prep(0)
    for bi in range(nblk):
        if bi + 1 < nblk:
            prep(bi + 1)
        scan_block(bi)
        finish(bi)


def _gla(g3, lg3, norm_w):
    b, seq, gw = g3.shape
    return pl.pallas_call(
        _gla_kernel,
        grid=(b,),
        in_specs=[
            pl.BlockSpec((1, seq, gw), lambda bi: (bi, 0, 0)),
            pl.BlockSpec((1, seq, GLA_KW), lambda bi: (bi, 0, 0)),
            _const_spec((1, GLA_V_DIM)),
        ],
        out_specs=pl.BlockSpec((1, seq, GLA_WIDTH), lambda bi: (bi, 0, 0)),
        out_shape=jax.ShapeDtypeStruct((b, seq, GLA_WIDTH), BF16),
        scratch_shapes=[
            pltpu.VMEM((seq, GLA_KW), BF16),
            pltpu.VMEM((GLA_HEADS, seq, 2 * GLA_K_DIM), BF16),
            pltpu.VMEM((seq // GLA_CHUNK, 8, GLA_KW), F32),
            pltpu.VMEM((seq, GLA_WIDTH), F32),
            pltpu.VMEM((GLA_HEADS, GLA_V_DIM, 2 * GLA_K_DIM), F32),
        ],
        compiler_params=pltpu.CompilerParams(
            dimension_semantics=("arbitrary",), vmem_limit_bytes=VMEM_LIMIT),
        name="gla",
    )(g3, lg3, norm_w)


def _mlp_kernel(tiles_per_seq, x_ref, od_ref, og_ref, wo_ref, ln2_ref, wup_ref, cw_ref, cb_ref,
                wdn_ref, lnf_ref, o_ref, x1_ref, h_ref, u_ref, act_ref, carry_ref):
    tm = x_ref.shape[0]
    halo = 8
    ck = FFN_CHUNK
    nch = D_FF // ck

    @pl.when(pl.program_id(0) % tiles_per_seq == 0)
    def _():
        carry_ref[...] = jnp.zeros_like(carry_ref)

    mix = jnp.concatenate([od_ref[...], og_ref[...]], axis=1)
    x1 = x_ref[...] + _dot(mix, wo_ref[...])
    x1_ref[...] = x1
    h_ref[...] = _rms(x1, ln2_ref[...]).astype(BF16)

    def gate_value(ref, c):
        return jnp.concatenate([ref[:, c * ck:(c + 1) * ck],
                                ref[:, D_FF + c * ck:D_FF + (c + 1) * ck]], axis=1)

    def up(c):
        par = c % 2
        u = _dot(h_ref[...], gate_value(wup_ref, c))
        u_ref[par, 0:halo, :] = carry_ref[c]
        u_ref[par, halo:halo + tm, :] = u
        carry_ref[c] = u[tm - halo:tm, :]

    def conv_act(c):
        par = c % 2
        w = gate_value(cw_ref, c)
        y = (u_ref[par, halo:halo + tm, :] * w[2:3, :]
             + u_ref[par, halo - 1:halo - 1 + tm, :] * w[1:2, :]
             + u_ref[par, halo - 2:halo - 2 + tm, :] * w[0:1, :]
             + gate_value(cb_ref, c))
        gate, val = y[:, :ck], y[:, ck:]
        act_ref[:, c * ck:(c + 1) * ck] = (gate / (1.0 + jnp.exp(-gate)) * val).astype(BF16)

    def down(c0, c1):
        return _dot(act_ref[:, c0 * ck:c1 * ck], wdn_ref[c0 * ck:c1 * ck, :])

    groups = [(g, min(g + DOWN_GROUP, nch)) for g in range(0, nch, DOWN_GROUP)]
    up(0)
    for c in range(nch):
        if c + 1 < nch:
            up(c + 1)
        for g0, g1 in groups[:-1]:
            if g1 == c:
                x1_ref[...] += down(g0, g1)
        conv_act(c)
    o_ref[...] = _rms(x1_ref[...] + down(*groups[-1]), lnf_ref[...])


def _mlp(x2, od, og, wo, ln2_w, wup, cw, cb, wdn, lnf_w, seq):
    n = x2.shape[0]
    tm = TM_FFN
    nch = D_FF // FFN_CHUNK
    tok = lambda w: pl.BlockSpec((tm, w), lambda i: (i, 0))
    return pl.pallas_call(
        functools.partial(_mlp_kernel, seq // tm),
        grid=(n // tm,),
        in_specs=[tok(D_MODEL), tok(DIFF_WIDTH), tok(GLA_WIDTH),
                  _const_spec(wo.shape), _const_spec((1, D_MODEL)),
                  _const_spec(wup.shape), _const_spec(cw.shape), _const_spec(cb.shape),
                  _const_spec(wdn.shape), _const_spec((1, D_MODEL))],
        out_specs=tok(D_MODEL),
        out_shape=jax.ShapeDtypeStruct((n, D_MODEL), F32),
        scratch_shapes=[
            pltpu.VMEM((tm, D_MODEL), F32),
            pltpu.VMEM((tm, D_MODEL), BF16),
            pltpu.VMEM((2, tm + 8, 2 * FFN_CHUNK), F32),
            pltpu.VMEM((tm, D_FF), BF16),
            pltpu.VMEM((nch, 8, 2 * FFN_CHUNK), F32),
        ],
        compiler_params=pltpu.CompilerParams(
            dimension_semantics=("arbitrary",), vmem_limit_bytes=VMEM_LIMIT),
        name="mlp",
    )(x2, od, og, wo, ln2_w, wup, cw, cb, wdn, lnf_w)


def kernel(x, ln1_w, w_in, diff_lq1, diff_lk1, diff_lq2, diff_lk2, diff_subln_w,
           gla_wg2, gla_bg, gla_norm_w, w_out, ln2_w, w_up, conv_w, conv_b,
           w_down, lnf_w):
    b, seq, d = x.shape
    n = b * seq
    depth = w_in.shape[0]
    assert depth == 1, depth
    x2 = x.reshape(n, d)
    qk_w = DIFF_HEADS * 2 * DIFF_QK_DIM
    c0 = 2 * qk_w + DIFF_WIDTH
    c1 = c0 + 2 * GLA_KW + GLA_WIDTH
    c2 = c1 + GLA_GATE_RANK
    for l in range(depth):
        lam_init = 0.8 - 0.6 * math.exp(-0.3 * l)
        w = w_in[l]
        scale = jnp.ones((c0 + IN_G + IN_R,), F32)
        scale = scale.at[:qk_w].set(DIFF_QK_DIM ** -0.5).at[c0:c0 + GLA_KW].set(GLA_K_DIM ** -0.5)
        w_all = (jnp.concatenate(
            [w[:, :c1], w[:, c2:], w[:, c1:c2], jnp.zeros((d, IN_R - GLA_GATE_RANK), F32)], axis=1)
            * scale).astype(BF16)
        wg2 = jnp.pad(gla_wg2[l], ((0, IN_R - GLA_GATE_RANK), (0, 0))).astype(BF16)

        a, g, lg, w_up16, w_down16, w_out16 = _in_proj(
            x2, ln1_w[l][None, :], w_all, wg2, gla_bg[l][None, :], w_up[l], w_down[l], w_out[l])
        o_diff = _diff_attn(a.reshape(b, seq, -1),
                            diff_lq1[l][None, :], diff_lk1[l][None, :],
                            diff_lq2[l][None, :], diff_lk2[l][None, :],
                            diff_subln_w[l][:, None], lam_init)
        o_gla = _gla(g.reshape(b, seq, -1), lg.reshape(b, seq, -1), gla_norm_w[l][None, :])

        x2 = _mlp(x2, o_diff.reshape(n, -1), o_gla.reshape(n, -1), w_out16,
                  ln2_w[l][None, :], w_up16, conv_w[l], conv_b[l][None, :],
                  w_down16, lnf_w[None, :], seq)
    return x2.reshape(b, seq, d)
```

```python
import functools
import math

import jax
import jax.numpy as jnp
from jax import lax
from jax.experimental import pallas as pl
from jax.experimental.pallas import tpu as pltpu

F32 = jnp.float32
BF16 = jnp.bfloat16

D_MODEL = 1024
DIFF_HEADS = 4
DIFF_QK_DIM = 64
DIFF_V_DIM = 128
DIFF_WIDTH = DIFF_HEADS * DIFF_V_DIM
GLA_HEADS = 4
GLA_K_DIM = 64
GLA_V_DIM = 128
GLA_KW = GLA_HEADS * GLA_K_DIM
GLA_WIDTH = GLA_HEADS * GLA_V_DIM
GLA_GATE_RANK = 16
GLA_TAU = 16.0
GLA_CHUNK = 64
GLA_BLK = 256
D_FF = 2816
EPS = 1e-6

LANES = 128
VMEM_LIMIT = 56 * 1024 * 1024

TM_PROJ = 512
TM_FFN = 512
FFN_CHUNK = 256
DOWN_GROUP = 4
TQ = 256
TK = 256
ROW_BLK = 64


def _dot(a, b):
    return jnp.dot(a, b, preferred_element_type=F32)


def _dot_nt(a, b):
    return lax.dot_general(a, b, (((1,), (1,)), ((), ())), preferred_element_type=F32)


def _dot_tn(a, b):
    return lax.dot_general(a, b, (((0,), (0,)), ((), ())), preferred_element_type=F32)


def _rms(x, w):
    ms = jnp.mean(x * x, axis=-1, keepdims=True)
    return x * lax.rsqrt(ms + EPS) * w


def _const_spec(shape):
    nd = len(shape)
    return pl.BlockSpec(shape, lambda *_: (0,) * nd, pipeline_mode=pl.Buffered(1))


IN_A = 2 * DIFF_HEADS * 2 * DIFF_QK_DIM + DIFF_WIDTH
IN_G = 2 * GLA_KW + 2 * GLA_WIDTH
IN_R = LANES


def _in_proj_kernel(x_ref, ln_ref, w_ref, wg2_ref, bg_ref, wup_ref, wdn_ref, wo_ref,
                    a_ref, g_ref, lg_ref, wup16_ref, wdn16_ref, wo16_ref):
    h = _rms(x_ref[...], ln_ref[...]).astype(BF16)
    a_ref[...] = _dot_nt(h, w_ref[0:IN_A, :]).astype(BF16)
    gr = _dot_nt(h, w_ref[IN_A:IN_A + IN_G + IN_R, :])
    g_ref[...] = gr[:, 0:IN_G].astype(BF16)
    r = gr[:, IN_G:IN_G + IN_R].astype(BF16)
    logits = _dot(r, wg2_ref[...]) + bg_ref[...]
    ls = jnp.minimum(logits, 0.0) - jnp.log(1.0 + jnp.exp(-jnp.abs(logits)))
    lg_ref[...] = ls * (1.0 / GLA_TAU)
    wup16_ref[...] = wup_ref[...].astype(BF16)
    wdn16_ref[...] = wdn_ref[...].astype(BF16)
    wo16_ref[...] = wo_ref[...].astype(BF16)


def _in_proj(x2, ln1_w, w_all, wg2, bg, w_up, w_down, w_out):
    n = x2.shape[0]
    tm = TM_PROJ
    steps = n // tm
    up_rows = w_up.shape[0] // steps
    out_rows = w_out.shape[0] // steps
    dn_rows = 2 * w_down.shape[0] // steps
    assert up_rows * steps == w_up.shape[0] and out_rows * steps == w_out.shape[0]
    assert dn_rows * steps == 2 * w_down.shape[0]
    assert up_rows % 16 == 0 and out_rows % 16 == 0 and dn_rows % 16 == 0
    slab = lambda rows, w, every: pl.BlockSpec((rows, w), lambda i: (i // every, 0))
    w_specs = [slab(up_rows, w_up.shape[1], 1), slab(dn_rows, w_down.shape[1], 2),
               slab(out_rows, w_out.shape[1], 1)]
    return pl.pallas_call(
        _in_proj_kernel,
        grid=(steps,),
        in_specs=[
            pl.BlockSpec((tm, D_MODEL), lambda i: (i, 0)),
            _const_spec((1, D_MODEL)),
            _const_spec(w_all.shape),
            _const_spec(wg2.shape),
            _const_spec((1, GLA_KW)),
        ] + w_specs,
        out_specs=[
            pl.BlockSpec((tm, IN_A), lambda i: (i, 0)),
            pl.BlockSpec((tm, IN_G), lambda i: (i, 0)),
            pl.BlockSpec((tm, GLA_KW), lambda i: (i, 0)),
        ] + w_specs,
        out_shape=[
            jax.ShapeDtypeStruct((n, IN_A), BF16),
            jax.ShapeDtypeStruct((n, IN_G), BF16),
            jax.ShapeDtypeStruct((n, GLA_KW), F32),
            jax.ShapeDtypeStruct(w_up.shape, BF16),
            jax.ShapeDtypeStruct(w_down.shape, BF16),
            jax.ShapeDtypeStruct(w_out.shape, BF16),
        ],
        compiler_params=pltpu.CompilerParams(
            dimension_semantics=("arbitrary",), vmem_limit_bytes=VMEM_LIMIT),
        name="in_proj",
    )(x2, ln1_w, w_all, wg2, bg, w_up, w_down, w_out)


def _diff_attn_kernel(lam_init, lq1_ref, lk1_ref, lq2_ref, lk2_ref, sw_ref,
                      q_ref, k_ref, v_ref, o_ref,
                      vt_ref, bias_ref, s_ref4, p_ref4):
    assert TQ == TK
    seq = q_ref.shape[1]
    nq = seq // TQ
    nk = seq // TK

    lam = (jnp.exp(jnp.sum(lq1_ref[...] * lk1_ref[...], axis=-1, keepdims=True))
           - jnp.exp(jnp.sum(lq2_ref[...] * lk2_ref[...], axis=-1, keepdims=True))
           + lam_init)

    for t in range(nk):
        vt_ref[:, t * TK:(t + 1) * TK] = v_ref[0, t * TK:(t + 1) * TK, :].astype(F32).T.astype(BF16)

    row = lax.broadcasted_iota(jnp.int32, (2 * DIFF_QK_DIM, TQ), 0)
    krow = lax.broadcasted_iota(jnp.int32, (TQ, 2 * TQ), 0)
    qcol = lax.broadcasted_iota(jnp.int32, (TQ, 2 * TQ), 1) % TQ
    bias_ref[...] = jnp.where(krow <= qcol, 0.0, -jnp.inf).astype(F32)

    def fold8(x, op):
        parts = [x[a * 8:(a + 1) * 8, :] for a in range(x.shape[0] // 8)]
        while len(parts) > 1:
            parts = [op(parts[a], parts[a + 1]) for a in range(0, len(parts), 2)]
        return parts[0]

    def scores(i):
        nkeys = (i + 1) * TQ
        qt = q_ref[0, i * TQ:(i + 1) * TQ, :].astype(F32).T.astype(BF16)
        zero = jnp.zeros_like(qt)
        q12t = jnp.concatenate([jnp.where(row < DIFF_QK_DIM, qt, zero),
                                jnp.where(row >= DIFF_QK_DIM, qt, zero)], axis=1)
        s_ref4[i % 2, 0:nkeys, :] = _dot(k_ref[0, 0:nkeys, :], q12t)

    def softmax(i):
        nblk = (i + 1) * TQ // ROW_BLK

        def block(r):
            blk = s_ref4[i % 2, r * ROW_BLK:(r + 1) * ROW_BLK, :]
            d = r * ROW_BLK - i * TQ
            if d >= 0:
                blk = blk + bias_ref[d:d + ROW_BLK, :]
            return blk

        m8 = fold8(block(0), jnp.maximum)
        for r in range(1, nblk):
            m8 = jnp.maximum(m8, fold8(block(r), jnp.maximum))
        m = jnp.max(m8, axis=0, keepdims=True)
        mb = jnp.broadcast_to(m, (ROW_BLK, 2 * TQ))
        l8 = jnp.zeros((8, 2 * TQ), F32)
        for r in range(nblk):
            p = jnp.exp(block(r) - mb)
            l8 = l8 + fold8(p, jnp.add)
            p_ref4[i % 2, r * ROW_BLK:(r + 1) * ROW_BLK, :] = p.astype(BF16)
        return jnp.sum(l8, axis=0, keepdims=True)

    def values(i, l):
        nkeys = (i + 1) * TQ
        acc = _dot(vt_ref[:, 0:nkeys], p_ref4[i % 2, 0:nkeys, :]) / l
        ot = acc[:, :TQ] - lam * acc[:, TQ:]
        ms = jnp.mean(ot * ot, axis=0, keepdims=True)
        ot = ot * lax.rsqrt(ms + EPS) * sw_ref[...] * (1.0 - lam_init)
        o_ref[0, i * TQ:(i + 1) * TQ, :] = ot.T.astype(BF16)

    scores(0)
    sums = {}
    for i in range(nq):
        if i > 0:
            values(i - 1, sums.pop(i - 1))
        if i + 1 < nq:
            scores(i + 1)
        sums[i] = softmax(i)
    values(nq - 1, sums.pop(nq - 1))


def _diff_attn(a3, lq1, lk1, lq2, lk2, subln_col, lam_init):
    b, seq, _ = a3.shape
    hd = 2 * DIFF_QK_DIM
    vec = _const_spec((1, DIFF_QK_DIM))
    return pl.pallas_call(
        functools.partial(_diff_attn_kernel, lam_init),
        grid=(b, DIFF_HEADS),
        in_specs=[
            vec, vec, vec, vec,
            _const_spec((DIFF_V_DIM, 1)),
            pl.BlockSpec((1, seq, hd), lambda bi, h: (bi, 0, h)),
            pl.BlockSpec((1, seq, hd), lambda bi, h: (bi, 0, DIFF_HEADS + h)),
            pl.BlockSpec((1, seq, DIFF_V_DIM), lambda bi, h: (bi, 0, 2 * DIFF_HEADS + h)),
        ],
        out_specs=pl.BlockSpec((1, seq, DIFF_V_DIM), lambda bi, h: (bi, 0, h)),
        out_shape=jax.ShapeDtypeStruct((b, seq, DIFF_WIDTH), BF16),
        scratch_shapes=[
            pltpu.VMEM((DIFF_V_DIM, seq), BF16),
            pltpu.VMEM((TQ, 2 * TQ), F32),
            pltpu.VMEM((2, seq, 2 * TQ), F32),
            pltpu.VMEM((2, seq, 2 * TQ), BF16),
        ],
        compiler_params=pltpu.CompilerParams(
            dimension_semantics=("arbitrary", "arbitrary"), vmem_limit_bytes=VMEM_LIMIT),
        name="diff_attn",
    )(lq1, lk1, lq2, lk2, subln_col, a3, a3, a3)


def _gla_kernel(g_ref, lg_ref, nw_ref, o_ref, qd_ref, keh_ref, dec_ref, oi_ref, st_ref):
    c = GLA_CHUNK
    blk = GLA_BLK
    cpb = blk // c
    seq = g_ref.shape[1]
    kw, vw, dv = GLA_KW, GLA_WIDTH, GLA_V_DIM
    pw = 2 * GLA_K_DIM
    assert pw == LANES

    ri = lax.broadcasted_iota(jnp.int32, (blk, blk), 0)
    ci = lax.broadcasted_iota(jnp.int32, (blk, blk), 1)
    intra = (ri // c == ci // c) & (ri >= ci)
    tri = intra.astype(BF16)
    lane_half = lax.broadcasted_iota(jnp.int32, (blk, pw), 1) // GLA_K_DIM
    row_half = lax.broadcasted_iota(jnp.int32, (pw, blk), 0) // GLA_K_DIM

    def pair(h):
        p = h // 2
        return slice(p * pw, (p + 1) * pw)

    def vcols(h):
        return slice(h * dv, (h + 1) * dv)

    st_ref[...] = jnp.zeros_like(st_ref)

    def span(start, size):
        if isinstance(start, int):
            return slice(start, start + size)
        return pl.ds(pl.multiple_of(start, size), size)

    def prep(bi):
        rows = span(bi * blk, blk)
        lg = lg_ref[0, rows, :]
        hi = lg.astype(BF16)
        lo = (lg - hi.astype(F32)).astype(BF16)
        b = _dot(tri, hi) + _dot(tri, lo)
        tot = jnp.concatenate(
            [jnp.broadcast_to(b[j * c + c - 1:(j + 1) * c, :], (c, kw)) for j in range(cpb)], axis=0)
        g = g_ref[0, rows, :]
        q = g[:, 0:kw].astype(F32)
        k = g[:, kw:2 * kw].astype(F32)
        qd = (q * jnp.exp(b)).astype(BF16)
        kd = k * jnp.exp(-b)
        ke = (k * jnp.exp(tot - b)).astype(BF16)
        qd_ref[rows, :] = qd
        dec = jnp.exp(tot)
        for j in range(cpb):
            dec_ref[bi * cpb + j] = dec[j * c:j * c + 8, :]
        kdt = kd.T.astype(BF16)
        for h in range(GLA_HEADS):
            keh_ref[h, rows, :] = jnp.where(lane_half == h % 2, ke[:, pair(h)], jnp.zeros((), BF16))
            kdt_h = jnp.where(row_half == h % 2, kdt[pair(h), :], jnp.zeros((), BF16))
            a = jnp.where(intra, _dot(qd[:, pair(h)], kdt_h), 0.0).astype(BF16)
            oi_ref[rows, vcols(h)] = _dot(a, g[:, 2 * kw + h * dv:2 * kw + (h + 1) * dv])

    def scan(n):
        rows = span(n * c, c)
        for h in range(GLA_HEADS):
            st = st_ref[h]
            oi_ref[rows, vcols(h)] += _dot_nt(qd_ref[rows, pair(h)], st.astype(BF16))
            v_h = g_ref[0, rows, 2 * kw + h * dv:2 * kw + (h + 1) * dv]
            st_ref[h] = st * dec_ref[n, 0:1, pair(h)] + _dot_tn(v_h, keh_ref[h, rows, :])

    def scan_block(bi):
        for j in range(cpb):
            scan(bi * cpb + j)

    def finish(bi):
        rows = span(bi * blk, blk)
        nw = nw_ref[...]
        for h in range(GLA_HEADS):
            gate = g_ref[0, rows, 2 * kw + vw + h * dv:2 * kw + vw + (h + 1) * dv].astype(F32)
            y = _rms(oi_ref[rows, vcols(h)], nw) * (gate / (1.0 + jnp.exp(-gate)))
            o_ref[0, rows, vcols(h)] = y.astype(BF16)

    nblk = seq // blk
    prep(0)
    for bi in range(nblk):
        if bi + 1 < nblk:
            prep(bi + 1)
        scan_block(bi)
        finish(bi)


def _gla(g3, lg3, norm_w):
    b, seq, gw = g3.shape
    return pl.pallas_call(
        _gla_kernel,
        grid=(b,),
        in_specs=[
            pl.BlockSpec((1, seq, gw), lambda bi: (bi, 0, 0)),
            pl.BlockSpec((1, seq, GLA_KW), lambda bi: (bi, 0, 0)),
            _const_spec((1, GLA_V_DIM)),
        ],
        out_specs=pl.BlockSpec((1, seq, GLA_WIDTH), lambda bi: (bi, 0, 0)),
        out_shape=jax.ShapeDtypeStruct((b, seq, GLA_WIDTH), BF16),
        scratch_shapes=[
            pltpu.VMEM((seq, GLA_KW), BF16),
            pltpu.VMEM((GLA_HEADS, seq, 2 * GLA_K_DIM), BF16),
            pltpu.VMEM((seq // GLA_CHUNK, 8, GLA_KW), F32),
            pltpu.VMEM((seq, GLA_WIDTH), F32),
            pltpu.VMEM((GLA_HEADS, GLA_V_DIM, 2 * GLA_K_DIM), F32),
        ],
        compiler_params=pltpu.CompilerParams(
            dimension_semantics=("arbitrary",), vmem_limit_bytes=VMEM_LIMIT),
        name="gla",
    )(g3, lg3, norm_w)


def _mlp_kernel(tiles_per_seq, x_ref, od_ref, og_ref, wo_ref, ln2_ref, wup_ref, cw_ref, cb_ref,
                wdn_ref, lnf_ref, o_ref, x1_ref, h_ref, u_ref, act_ref, carry_ref):
    tm = x_ref.shape[0]
    halo = 8
    ck = FFN_CHUNK
    nch = D_FF // ck

    @pl.when(pl.program_id(0) % tiles_per_seq == 0)
    def _():
        carry_ref[...] = jnp.zeros_like(carry_ref)

    mix = jnp.concatenate([od_ref[...], og_ref[...]], axis=1)
    x1 = x_ref[...] + _dot(mix, wo_ref[...])
    x1_ref[...] = x1
    h_ref[...] = _rms(x1, ln2_ref[...]).astype(BF16)

    def gate_value(ref, c):
        return jnp.concatenate([ref[:, c * ck:(c + 1) * ck],
                                ref[:, D_FF + c * ck:D_FF + (c + 1) * ck]], axis=1)

    def up(c):
        par = c % 2
        u = _dot(h_ref[...], gate_value(wup_ref, c))
        u_ref[par, 0:halo, :] = carry_ref[c]
        u_ref[par, halo:halo + tm, :] = u
        carry_ref[c] = u[tm - halo:tm, :]

    def conv_act(c):
        par = c % 2
        w = gate_value(cw_ref, c)
        y = (u_ref[par, halo:halo + tm, :] * w[2:3, :]
             + u_ref[par, halo - 1:halo - 1 + tm, :] * w[1:2, :]
             + u_ref[par, halo - 2:halo - 2 + tm, :] * w[0:1, :]
             + gate_value(cb_ref, c))
        gate, val = y[:, :ck], y[:, ck:]
        act_ref[:, c * ck:(c + 1) * ck] = (gate / (1.0 + jnp.exp(-gate)) * val).astype(BF16)

    def down(c0, c1):
        return _dot(act_ref[:, c0 * ck:c1 * ck], wdn_ref[c0 * ck:c1 * ck, :])

    groups = [(g, min(g + DOWN_GROUP, nch)) for g in range(0, nch, DOWN_GROUP)]
    up(0)
    for c in range(nch):
        if c + 1 < nch:
            up(c + 1)
        for g0, g1 in groups[:-1]:
            if g1 == c:
                x1_ref[...] += down(g0, g1)
        conv_act(c)
    o_ref[...] = _rms(x1_ref[...] + down(*groups[-1]), lnf_ref[...])


def _mlp(x2, od, og, wo, ln2_w, wup, cw, cb, wdn, lnf_w, seq):
    n = x2.shape[0]
    tm = TM_FFN
    nch = D_FF // FFN_CHUNK
    tok = lambda w: pl.BlockSpec((tm, w), lambda i: (i, 0))
    return pl.pallas_call(
        functools.partial(_mlp_kernel, seq // tm),
        grid=(n // tm,),
        in_specs=[tok(D_MODEL), tok(DIFF_WIDTH), tok(GLA_WIDTH),
                  _const_spec(wo.shape), _const_spec((1, D_MODEL)),
                  _const_spec(wup.shape), _const_spec(cw.shape), _const_spec(cb.shape),
                  _const_spec(wdn.shape), _const_spec((1, D_MODEL))],
        out_specs=tok(D_MODEL),
        out_shape=jax.ShapeDtypeStruct((n, D_MODEL), F32),
        scratch_shapes=[
            pltpu.VMEM((tm, D_MODEL), F32),
            pltpu.VMEM((tm, D_MODEL), BF16),
            pltpu.VMEM((2, tm + 8, 2 * FFN_CHUNK), F32),
            pltpu.VMEM((tm, D_FF), BF16),
            pltpu.VMEM((nch, 8, 2 * FFN_CHUNK), F32),
        ],
        compiler_params=pltpu.CompilerParams(
            dimension_semantics=("arbitrary",), vmem_limit_bytes=VMEM_LIMIT),
        name="mlp",
    )(x2, od, og, wo, ln2_w, wup, cw, cb, wdn, lnf_w)


def kernel(x, ln1_w, w_in, diff_lq1, diff_lk1, diff_lq2, diff_lk2, diff_subln_w,
           gla_wg2, gla_bg, gla_norm_w, w_out, ln2_w, w_up, conv_w, conv_b,
           w_down, lnf_w):
    b, seq, d = x.shape
    n = b * seq
    depth = w_in.shape[0]
    assert depth == 1, depth
    x2 = x.reshape(n, d)
    qk_w = DIFF_HEADS * 2 * DIFF_QK_DIM
    c0 = 2 * qk_w + DIFF_WIDTH
    c1 = c0 + 2 * GLA_KW + GLA_WIDTH
    c2 = c1 + GLA_GATE_RANK
    for l in range(depth):
        lam_init = 0.8 - 0.6 * math.exp(-0.3 * l)
        wt = jnp.swapaxes(w_in[l], 0, 1)
        scale = jnp.ones((c0 + IN_G + IN_R, 1), F32)
        scale = scale.at[:qk_w].set(DIFF_QK_DIM ** -0.5).at[c0:c0 + GLA_KW].set(GLA_K_DIM ** -0.5)
        w_all = (jnp.concatenate(
            [wt[:c1], wt[c2:], wt[c1:c2], jnp.zeros((IN_R - GLA_GATE_RANK, d), F32)], axis=0)
            * scale).astype(BF16)
        wg2 = jnp.pad(gla_wg2[l], ((0, IN_R - GLA_GATE_RANK), (0, 0))).astype(BF16)

        a, g, lg, w_up16, w_down16, w_out16 = _in_proj(
            x2, ln1_w[l][None, :], w_all, wg2, gla_bg[l][None, :], w_up[l], w_down[l], w_out[l])
        o_diff = _diff_attn(a.reshape(b, seq, -1),
                            diff_lq1[l][None, :], diff_lk1[l][None, :],
                            diff_lq2[l][None, :], diff_lk2[l][None, :],
                            diff_subln_w[l][:, None], lam_init)
        o_gla = _gla(g.reshape(b, seq, -1), lg.reshape(b, seq, -1), gla_norm_w[l][None, :])

        x2 = _mlp(x2, o_diff.reshape(n, -1), o_gla.reshape(n, -1), w_out16,
                  ln2_w[l][None, :], w_up16, conv_w[l], conv_b[l][None, :],
                  w_down16, lnf_w[None, :], seq)
    return x2.reshape(b, seq, d)
```

```python
import functools
import math

import jax
import jax.numpy as jnp
from jax import lax
from jax.experimental import pallas as pl
from jax.experimental.pallas import tpu as pltpu

F32 = jnp.float32
BF16 = jnp.bfloat16

D_MODEL = 1024
DIFF_HEADS = 4
DIFF_QK_DIM = 64
DIFF_V_DIM = 128
DIFF_WIDTH = DIFF_HEADS * DIFF_V_DIM
GLA_HEADS = 4
GLA_K_DIM = 64
GLA_V_DIM = 128
GLA_KW = GLA_HEADS * GLA_K_DIM
GLA_WIDTH = GLA_HEADS * GLA_V_DIM
GLA_GATE_RANK = 16
GLA_TAU = 16.0
GLA_CHUNK = 64
GLA_BLK = 256
D_FF = 2816
EPS = 1e-6

LANES = 128
VMEM_LIMIT = 62 * 1024 * 1024

TM_PROJ = 1024
TM_FFN = 1024
FFN_CHUNK = 256
DOWN_GROUP = 4
TQ = 256
TK = 256
ROW_BLK = 64
ATTN_HEADS_PER_STEP = 2


def _dot(a, b):
    return jnp.dot(a, b, preferred_element_type=F32)


def _dot_nt(a, b):
    return lax.dot_general(a, b, (((1,), (1,)), ((), ())), preferred_element_type=F32)


def _dot_tn(a, b):
    return lax.dot_general(a, b, (((0,), (0,)), ((), ())), preferred_element_type=F32)


def _rms(x, w):
    ms = jnp.mean(x * x, axis=-1, keepdims=True)
    return x * lax.rsqrt(ms + EPS) * w


def _const_spec(shape):
    nd = len(shape)
    return pl.BlockSpec(shape, lambda *_: (0,) * nd, pipeline_mode=pl.Buffered(1))


IN_A = 2 * DIFF_HEADS * 2 * DIFF_QK_DIM + DIFF_WIDTH
IN_G = 2 * GLA_KW + 2 * GLA_WIDTH
IN_R = LANES


def _in_proj_kernel(x_ref, ln_ref, w_ref, wg2_ref, bg_ref, wup_ref, wdn_ref, wo_ref,
                    a_ref, g_ref, lg_ref, wup16_ref, wdn16_ref, wo16_ref):
    h = _rms(x_ref[...], ln_ref[...]).astype(BF16)
    a_ref[...] = _dot_nt(h, w_ref[0:IN_A, :]).astype(BF16)
    gr = _dot_nt(h, w_ref[IN_A:IN_A + IN_G + IN_R, :])
    g_ref[...] = gr[:, 0:IN_G].astype(BF16)
    r = gr[:, IN_G:IN_G + IN_R].astype(BF16)
    logits = _dot(r, wg2_ref[...]) + bg_ref[...]
    ls = jnp.minimum(logits, 0.0) - jnp.log(1.0 + jnp.exp(-jnp.abs(logits)))
    lg_ref[...] = ls * (1.0 / GLA_TAU)
    wup16_ref[...] = wup_ref[...].astype(BF16)
    wdn16_ref[...] = wdn_ref[...].astype(BF16)
    wo16_ref[...] = wo_ref[...].astype(BF16)


def _in_proj(x2, ln1_w, w_all, wg2, bg, w_up, w_down, w_out):
    n = x2.shape[0]
    tm = TM_PROJ
    steps = n // tm
    up_rows = w_up.shape[0] // steps
    out_rows = w_out.shape[0] // steps
    dn_rows = 2 * w_down.shape[0] // steps
    assert up_rows * steps == w_up.shape[0] and out_rows * steps == w_out.shape[0]
    assert dn_rows * steps == 2 * w_down.shape[0]
    assert up_rows % 16 == 0 and out_rows % 16 == 0 and dn_rows % 16 == 0
    slab = lambda rows, w, every: pl.BlockSpec((rows, w), lambda i: (i // every, 0))
    w_specs = [slab(up_rows, w_up.shape[1], 1), slab(dn_rows, w_down.shape[1], 2),
               slab(out_rows, w_out.shape[1], 1)]
    return pl.pallas_call(
        _in_proj_kernel,
        grid=(steps,),
        in_specs=[
            pl.BlockSpec((tm, D_MODEL), lambda i: (i, 0)),
            _const_spec((1, D_MODEL)),
            _const_spec(w_all.shape),
            _const_spec(wg2.shape),
            _const_spec((1, GLA_KW)),
        ] + w_specs,
        out_specs=[
            pl.BlockSpec((tm, IN_A), lambda i: (i, 0)),
            pl.BlockSpec((tm, IN_G), lambda i: (i, 0)),
            pl.BlockSpec((tm, GLA_KW), lambda i: (i, 0)),
        ] + w_specs,
        out_shape=[
            jax.ShapeDtypeStruct((n, IN_A), BF16),
            jax.ShapeDtypeStruct((n, IN_G), BF16),
            jax.ShapeDtypeStruct((n, GLA_KW), F32),
            jax.ShapeDtypeStruct(w_up.shape, BF16),
            jax.ShapeDtypeStruct(w_down.shape, BF16),
            jax.ShapeDtypeStruct(w_out.shape, BF16),
        ],
        compiler_params=pltpu.CompilerParams(
            dimension_semantics=("arbitrary",), vmem_limit_bytes=VMEM_LIMIT),
        name="in_proj",
    )(x2, ln1_w, w_all, wg2, bg, w_up, w_down, w_out)


def _diff_attn_kernel(lam_init, lq1_ref, lk1_ref, lq2_ref, lk2_ref, sw_ref,
                      q_ref, k_ref, v_ref, o_ref,
                      vt_ref, bias_ref, s_ref4, p_ref4):
    assert TQ == TK
    seq = q_ref.shape[1]
    nq = seq // TQ
    nk = seq // TK

    lam = (jnp.exp(jnp.sum(lq1_ref[...] * lk1_ref[...], axis=-1, keepdims=True))
           - jnp.exp(jnp.sum(lq2_ref[...] * lk2_ref[...], axis=-1, keepdims=True))
           + lam_init)

    hd = 2 * DIFF_QK_DIM
    heads = q_ref.shape[2] // hd
    for hh in range(heads):
        for t in range(nk):
            vt_ref[hh, :, t * TK:(t + 1) * TK] = (
                v_ref[0, t * TK:(t + 1) * TK, hh * DIFF_V_DIM:(hh + 1) * DIFF_V_DIM]
                .astype(F32).T.astype(BF16))

    row = lax.broadcasted_iota(jnp.int32, (2 * DIFF_QK_DIM, TQ), 0)
    krow = lax.broadcasted_iota(jnp.int32, (TQ, 2 * TQ), 0)
    qcol = lax.broadcasted_iota(jnp.int32, (TQ, 2 * TQ), 1) % TQ
    bias_ref[...] = jnp.where(krow <= qcol, 0.0, -jnp.inf).astype(F32)

    def fold8(x, op):
        parts = [x[a * 8:(a + 1) * 8, :] for a in range(x.shape[0] // 8)]
        while len(parts) > 1:
            parts = [op(parts[a], parts[a + 1]) for a in range(0, len(parts), 2)]
        return parts[0]

    def scores(w):
        hh, i = divmod(w, nq)
        nkeys = (i + 1) * TQ
        qt = q_ref[0, i * TQ:(i + 1) * TQ, hh * hd:(hh + 1) * hd].astype(F32).T.astype(BF16)
        zero = jnp.zeros_like(qt)
        q12t = jnp.concatenate([jnp.where(row < DIFF_QK_DIM, qt, zero),
                                jnp.where(row >= DIFF_QK_DIM, qt, zero)], axis=1)
        s_ref4[w % 2, 0:nkeys, :] = _dot(k_ref[0, 0:nkeys, hh * hd:(hh + 1) * hd], q12t)

    def softmax(w):
        i = w % nq
        nblk = (i + 1) * TQ // ROW_BLK

        def block(r):
            blk = s_ref4[w % 2, r * ROW_BLK:(r + 1) * ROW_BLK, :]
            d = r * ROW_BLK - i * TQ
            if d >= 0:
                blk = blk + bias_ref[d:d + ROW_BLK, :]
            return blk

        m8 = fold8(block(0), jnp.maximum)
        for r in range(1, nblk):
            m8 = jnp.maximum(m8, fold8(block(r), jnp.maximum))
        m = jnp.max(m8, axis=0, keepdims=True)
        mb = jnp.broadcast_to(m, (ROW_BLK, 2 * TQ))
        l8 = jnp.zeros((8, 2 * TQ), F32)
        for r in range(nblk):
            p = jnp.exp(block(r) - mb)
            l8 = l8 + fold8(p, jnp.add)
            p_ref4[w % 2, r * ROW_BLK:(r + 1) * ROW_BLK, :] = p.astype(BF16)
        return jnp.sum(l8, axis=0, keepdims=True)

    def values(w, l):
        hh, i = divmod(w, nq)
        nkeys = (i + 1) * TQ
        acc = _dot(vt_ref[hh, :, 0:nkeys], p_ref4[w % 2, 0:nkeys, :]) / l
        ot = acc[:, :TQ] - lam * acc[:, TQ:]
        ms = jnp.mean(ot * ot, axis=0, keepdims=True)
        ot = ot * lax.rsqrt(ms + EPS) * sw_ref[...] * (1.0 - lam_init)
        o_ref[0, i * TQ:(i + 1) * TQ, hh * DIFF_V_DIM:(hh + 1) * DIFF_V_DIM] = ot.T.astype(BF16)

    nwork = heads * nq
    scores(0)
    sums = {}
    for w in range(nwork):
        if w > 0:
            values(w - 1, sums.pop(w - 1))
        if w + 1 < nwork:
            scores(w + 1)
        sums[w] = softmax(w)
    values(nwork - 1, sums.pop(nwork - 1))


def _diff_attn(a3, lq1, lk1, lq2, lk2, subln_col, lam_init):
    b, seq, _ = a3.shape
    hd = 2 * DIFF_QK_DIM
    assert hd == DIFF_V_DIM
    hps = ATTN_HEADS_PER_STEP
    groups = DIFF_HEADS // hps
    vec = _const_spec((1, DIFF_QK_DIM))
    return pl.pallas_call(
        functools.partial(_diff_attn_kernel, lam_init),
        grid=(b, groups),
        in_specs=[
            vec, vec, vec, vec,
            _const_spec((DIFF_V_DIM, 1)),
            pl.BlockSpec((1, seq, hps * hd), lambda bi, g: (bi, 0, g)),
            pl.BlockSpec((1, seq, hps * hd), lambda bi, g: (bi, 0, groups + g)),
            pl.BlockSpec((1, seq, hps * DIFF_V_DIM), lambda bi, g: (bi, 0, 2 * groups + g)),
        ],
        out_specs=pl.BlockSpec((1, seq, hps * DIFF_V_DIM), lambda bi, g: (bi, 0, g)),
        out_shape=jax.ShapeDtypeStruct((b, seq, DIFF_WIDTH), BF16),
        scratch_shapes=[
            pltpu.VMEM((hps, DIFF_V_DIM, seq), BF16),
            pltpu.VMEM((TQ, 2 * TQ), F32),
            pltpu.VMEM((2, seq, 2 * TQ), F32),
            pltpu.VMEM((2, seq, 2 * TQ), BF16),
        ],
        compiler_params=pltpu.CompilerParams(
            dimension_semantics=("arbitrary", "arbitrary"), vmem_limit_bytes=VMEM_LIMIT),
        name="diff_attn",
    )(lq1, lk1, lq2, lk2, subln_col, a3, a3, a3)


def _gla_kernel(g_ref, lg_ref, nw_ref, o_ref, qd_ref, keh_ref, dec_ref, oi_ref, st_ref):
    c = GLA_CHUNK
    blk = GLA_BLK
    cpb = blk // c
    seq = g_ref.shape[1]
    kw, vw, dv = GLA_KW, GLA_WIDTH, GLA_V_DIM
    pw = 2 * GLA_K_DIM
    assert pw == LANES

    ri = lax.broadcasted_iota(jnp.int32, (blk, blk), 0)
    ci = lax.broadcasted_iota(jnp.int32, (blk, blk), 1)
    intra = (ri // c == ci // c) & (ri >= ci)
    tri = intra.astype(BF16)
    lane_half = lax.broadcasted_iota(jnp.int32, (blk, pw), 1) // GLA_K_DIM
    row_half = lax.broadcasted_iota(jnp.int32, (pw, blk), 0) // GLA_K_DIM

    def pair(h):
        p = h // 2
        return slice(p * pw, (p + 1) * pw)

    def vcols(h):
        return slice(h * dv, (h + 1) * dv)

    st_ref[...] = jnp.zeros_like(st_ref)

    def span(start, size):
        if isinstance(start, int):
            return slice(start, start + size)
        return pl.ds(pl.multiple_of(start, size), size)

    def prep(bi):
        rows = span(bi * blk, blk)
        lg = lg_ref[0, rows, :]
        hi = lg.astype(BF16)
        lo = (lg - hi.astype(F32)).astype(BF16)
        b = _dot(tri, hi) + _dot(tri, lo)
        tot = jnp.concatenate(
            [jnp.broadcast_to(b[j * c + c - 1:(j + 1) * c, :], (c, kw)) for j in range(cpb)], axis=0)
        g = g_ref[0, rows, :]
        q = g[:, 0:kw].astype(F32)
        k = g[:, kw:2 * kw].astype(F32)
        qd = (q * jnp.exp(b)).astype(BF16)
        kd = k * jnp.exp(-b)
        ke = (k * jnp.exp(tot - b)).astype(BF16)
        qd_ref[rows, :] = qd
        dec = jnp.exp(tot)
        for j in range(cpb):
            dec_ref[bi * cpb + j] = dec[j * c:j * c + 8, :]
        kdt = kd.T.astype(BF16)
        for h in range(GLA_HEADS):
            keh_ref[h, rows, :] = jnp.where(lane_half == h % 2, ke[:, pair(h)], jnp.zeros((), BF16))
            kdt_h = jnp.where(row_half == h % 2, kdt[pair(h), :], jnp.zeros((), BF16))
            a = jnp.where(intra, _dot(qd[:, pair(h)], kdt_h), 0.0).astype(BF16)
            oi_ref[rows, vcols(h)] = _dot(a, g[:, 2 * kw + h * dv:2 * kw + (h + 1) * dv])

    def scan(n):
        rows = span(n * c, c)
        for h in range(GLA_HEADS):
            st = st_ref[h]
            oi_ref[rows, vcols(h)] += _dot_nt(qd_ref[rows, pair(h)], st.astype(BF16))
            v_h = g_ref[0, rows, 2 * kw + h * dv:2 * kw + (h + 1) * dv]
            st_ref[h] = st * dec_ref[n, 0:1, pair(h)] + _dot_tn(v_h, keh_ref[h, rows, :])

    def scan_block(bi):
        for j in range(cpb):
            scan(bi * cpb + j)

    def finish(bi):
        rows = span(bi * blk, blk)
        nw = nw_ref[...]
        for h in range(GLA_HEADS):
            gate = g_ref[0, rows, 2 * kw + vw + h * dv:2 * kw + vw + (h + 1) * dv].astype(F32)
            y = _rms(oi_ref[rows, vcols(h)], nw) * (gate / (1.0 + jnp.exp(-gate)))
            o_ref[0, rows, vcols(h)] = y.astype(BF16)

    nblk = seq // blk
    prep(0)
    for bi in range(nblk):
        if bi + 1 < nblk:
            prep(bi + 1)
        scan_block(bi)
        finish(bi)


def _gla(g3, lg3, norm_w):
    b, seq, gw = g3.shape
    return pl.pallas_call(
        _gla_kernel,
        grid=(b,),
        in_specs=[
            pl.BlockSpec((1, seq, gw), lambda bi: (bi, 0, 0)),
            pl.BlockSpec((1, seq, GLA_KW), lambda bi: (bi, 0, 0)),
            _const_spec((1, GLA_V_DIM)),
        ],
        out_specs=pl.BlockSpec((1, seq, GLA_WIDTH), lambda bi: (bi, 0, 0)),
        out_shape=jax.ShapeDtypeStruct((b, seq, GLA_WIDTH), BF16),
        scratch_shapes=[
            pltpu.VMEM((seq, GLA_KW), BF16),
            pltpu.VMEM((GLA_HEADS, seq, 2 * GLA_K_DIM), BF16),
            pltpu.VMEM((seq // GLA_CHUNK, 8, GLA_KW), F32),
            pltpu.VMEM((seq, GLA_WIDTH), F32),
            pltpu.VMEM((GLA_HEADS, GLA_V_DIM, 2 * GLA_K_DIM), F32),
        ],
        compiler_params=pltpu.CompilerParams(
            dimension_semantics=("arbitrary",), vmem_limit_bytes=VMEM_LIMIT),
        name="gla",
    )(g3, lg3, norm_w)


def _mlp_kernel(tiles_per_seq, x_ref, od_ref, og_ref, wo_ref, ln2_ref, wup_ref, cw_ref, cb_ref,
                wdn_ref, lnf_ref, o_ref, x1_ref, h_ref, u_ref, act_ref, carry_ref):
    tm = x_ref.shape[0]
    halo = 8
    ck = FFN_CHUNK
    nch = D_FF // ck

    @pl.when(pl.program_id(0) % tiles_per_seq == 0)
    def _():
        carry_ref[...] = jnp.zeros_like(carry_ref)

    mix = jnp.concatenate([od_ref[...], og_ref[...]], axis=1)
    x1 = x_ref[...] + _dot(mix, wo_ref[...])
    x1_ref[...] = x1
    h_ref[...] = _rms(x1, ln2_ref[...]).astype(BF16)

    def gate_value(ref, c):
        return jnp.concatenate([ref[:, c * ck:(c + 1) * ck],
                                ref[:, D_FF + c * ck:D_FF + (c + 1) * ck]], axis=1)

    def up(c):
        par = c % 2
        u = _dot(h_ref[...], gate_value(wup_ref, c))
        u_ref[par, 0:halo, :] = carry_ref[c]
        u_ref[par, halo:halo + tm, :] = u
        carry_ref[c] = u[tm - halo:tm, :]

    def conv_act(c):
        par = c % 2
        w = gate_value(cw_ref, c)
        y = (u_ref[par, halo:halo + tm, :] * w[2:3, :]
             + u_ref[par, halo - 1:halo - 1 + tm, :] * w[1:2, :]
             + u_ref[par, halo - 2:halo - 2 + tm, :] * w[0:1, :]
             + gate_value(cb_ref, c))
        gate, val = y[:, :ck], y[:, ck:]
        act_ref[:, c * ck:(c + 1) * ck] = (gate / (1.0 + jnp.exp(-gate)) * val).astype(BF16)

    def down(c0, c1):
        return _dot(act_ref[:, c0 * ck:c1 * ck], wdn_ref[c0 * ck:c1 * ck, :])

    groups = [(g, min(g + DOWN_GROUP, nch)) for g in range(0, nch, DOWN_GROUP)]
    up(0)
    for c in range(nch):
        if c + 1 < nch:
            up(c + 1)
        for g0, g1 in groups[:-1]:
            if g1 == c:
                x1_ref[...] += down(g0, g1)
        conv_act(c)
    o_ref[...] = _rms(x1_ref[...] + down(*groups[-1]), lnf_ref[...])


def _mlp(x2, od, og, wo, ln2_w, wup, cw, cb, wdn, lnf_w, seq):
    n = x2.shape[0]
    tm = TM_FFN
    nch = D_FF // FFN_CHUNK
    tok = lambda w: pl.BlockSpec((tm, w), lambda i: (i, 0))
    return pl.pallas_call(
        functools.partial(_mlp_kernel, seq // tm),
        grid=(n // tm,),
        in_specs=[tok(D_MODEL), tok(DIFF_WIDTH), tok(GLA_WIDTH),
                  _const_spec(wo.shape), _const_spec((1, D_MODEL)),
                  _const_spec(wup.shape), _const_spec(cw.shape), _const_spec(cb.shape),
                  _const_spec(wdn.shape), _const_spec((1, D_MODEL))],
        out_specs=tok(D_MODEL),
        out_shape=jax.ShapeDtypeStruct((n, D_MODEL), F32),
        scratch_shapes=[
            pltpu.VMEM((tm, D_MODEL), F32),
            pltpu.VMEM((tm, D_MODEL), BF16),
            pltpu.VMEM((2, tm + 8, 2 * FFN_CHUNK), F32),
            pltpu.VMEM((tm, D_FF), BF16),
            pltpu.VMEM((nch, 8, 2 * FFN_CHUNK), F32),
        ],
        compiler_params=pltpu.CompilerParams(
            dimension_semantics=("arbitrary",), vmem_limit_bytes=VMEM_LIMIT),
        name="mlp",
    )(x2, od, og, wo, ln2_w, wup, cw, cb, wdn, lnf_w)


def kernel(x, ln1_w, w_in, diff_lq1, diff_lk1, diff_lq2, diff_lk2, diff_subln_w,
           gla_wg2, gla_bg, gla_norm_w, w_out, ln2_w, w_up, conv_w, conv_b,
           w_down, lnf_w):
    b, seq, d = x.shape
    n = b * seq
    depth = w_in.shape[0]
    assert depth == 1, depth
    x2 = x.reshape(n, d)
    qk_w = DIFF_HEADS * 2 * DIFF_QK_DIM
    c0 = 2 * qk_w + DIFF_WIDTH
    c1 = c0 + 2 * GLA_KW + GLA_WIDTH
    c2 = c1 + GLA_GATE_RANK
    for l in range(depth):
        lam_init = 0.8 - 0.6 * math.exp(-0.3 * l)
        wt = jnp.swapaxes(w_in[l], 0, 1)
        scale = jnp.ones((c0 + IN_G + IN_R, 1), F32)
        scale = scale.at[:qk_w].set(DIFF_QK_DIM ** -0.5).at[c0:c0 + GLA_KW].set(GLA_K_DIM ** -0.5)
        w_all = (jnp.concatenate(
            [wt[:c1], wt[c2:], wt[c1:c2], jnp.zeros((IN_R - GLA_GATE_RANK, d), F32)], axis=0)
            * scale).astype(BF16)
        wg2 = jnp.pad(gla_wg2[l], ((0, IN_R - GLA_GATE_RANK), (0, 0))).astype(BF16)

        a, g, lg, w_up16, w_down16, w_out16 = _in_proj(
            x2, ln1_w[l][None, :], w_all, wg2, gla_bg[l][None, :], w_up[l], w_down[l], w_out[l])
        o_diff = _diff_attn(a.reshape(b, seq, -1),
                            diff_lq1[l][None, :], diff_lk1[l][None, :],
                            diff_lq2[l][None, :], diff_lk2[l][None, :],
                            diff_subln_w[l][:, None], lam_init)
        o_gla = _gla(g.reshape(b, seq, -1), lg.reshape(b, seq, -1), gla_norm_w[l][None, :])

        x2 = _mlp(x2, o_diff.reshape(n, -1), o_gla.reshape(n, -1), w_out16,
                  ln2_w[l][None, :], w_up16, conv_w[l], conv_b[l][None, :],
                  w_down16, lnf_w[None, :], seq)
    return x2.reshape(b, seq, d)
```

```python
import functools
import math

import jax
import jax.numpy as jnp
from jax import lax
from jax.experimental import pallas as pl
from jax.experimental.pallas import tpu as pltpu

F32 = jnp.float32
BF16 = jnp.bfloat16

D_MODEL = 1024
DIFF_HEADS = 4
DIFF_QK_DIM = 64
DIFF_V_DIM = 128
DIFF_WIDTH = DIFF_HEADS * DIFF_V_DIM
GLA_HEADS = 4
GLA_K_DIM = 64
GLA_V_DIM = 128
GLA_KW = GLA_HEADS * GLA_K_DIM
GLA_WIDTH = GLA_HEADS * GLA_V_DIM
GLA_GATE_RANK = 16
GLA_TAU = 16.0
GLA_CHUNK = 64
GLA_BLK = 256
D_FF = 2816
EPS = 1e-6

LANES = 128
VMEM_LIMIT = 62 * 1024 * 1024

TM_PROJ = 1024
TM_FFN = 512
FFN_CHUNK = 1408
DOWN_GROUP = 1
assert D_FF % FFN_CHUNK == 0 and FFN_CHUNK % LANES == 0
TQ = 256
TK = 256
ROW_BLK = 64
ATTN_HEADS_PER_STEP = 2


def _dot(a, b):
    return jnp.dot(a, b, preferred_element_type=F32)


def _dot_nt(a, b):
    return lax.dot_general(a, b, (((1,), (1,)), ((), ())), preferred_element_type=F32)


def _dot_tn(a, b):
    return lax.dot_general(a, b, (((0,), (0,)), ((), ())), preferred_element_type=F32)


def _rms(x, w):
    ms = jnp.mean(x * x, axis=-1, keepdims=True)
    return x * lax.rsqrt(ms + EPS) * w


def _const_spec(shape):
    nd = len(shape)
    return pl.BlockSpec(shape, lambda *_: (0,) * nd, pipeline_mode=pl.Buffered(1))


IN_A = 2 * DIFF_HEADS * 2 * DIFF_QK_DIM + DIFF_WIDTH
IN_G = 2 * GLA_KW + 2 * GLA_WIDTH
IN_R = LANES


def _in_proj_kernel(x_ref, ln_ref, w_ref, wg2_ref, bg_ref, wup_ref, wdn_ref, wo_ref,
                    a_ref, g_ref, lg_ref, wup16_ref, wdn16_ref, wo16_ref):
    h = _rms(x_ref[...], ln_ref[...]).astype(BF16)
    a_ref[...] = _dot_nt(h, w_ref[0:IN_A, :]).astype(BF16)
    gr = _dot_nt(h, w_ref[IN_A:IN_A + IN_G + IN_R, :])
    g_ref[...] = gr[:, 0:IN_G].astype(BF16)
    r = gr[:, IN_G:IN_G + IN_R].astype(BF16)
    logits = _dot(r, wg2_ref[...]) + bg_ref[...]
    ls = jnp.minimum(logits, 0.0) - jnp.log(1.0 + jnp.exp(-jnp.abs(logits)))
    lg_ref[...] = ls * (1.0 / GLA_TAU)
    wup16_ref[...] = wup_ref[...].astype(BF16)
    wdn16_ref[...] = wdn_ref[...].astype(BF16)
    wo16_ref[...] = wo_ref[...].astype(BF16)


def _in_proj(x2, ln1_w, w_all, wg2, bg, w_up, w_down, w_out):
    n = x2.shape[0]
    tm = TM_PROJ
    steps = n // tm
    up_rows = w_up.shape[0] // steps
    out_rows = w_out.shape[0] // steps
    dn_rows = 2 * w_down.shape[0] // steps
    assert up_rows * steps == w_up.shape[0] and out_rows * steps == w_out.shape[0]
    assert dn_rows * steps == 2 * w_down.shape[0]
    assert up_rows % 16 == 0 and out_rows % 16 == 0 and dn_rows % 16 == 0
    slab = lambda rows, w, every: pl.BlockSpec((rows, w), lambda i: (i // every, 0))
    w_specs = [slab(up_rows, w_up.shape[1], 1), slab(dn_rows, w_down.shape[1], 2),
               slab(out_rows, w_out.shape[1], 1)]
    return pl.pallas_call(
        _in_proj_kernel,
        grid=(steps,),
        in_specs=[
            pl.BlockSpec((tm, D_MODEL), lambda i: (i, 0)),
            _const_spec((1, D_MODEL)),
            _const_spec(w_all.shape),
            _const_spec(wg2.shape),
            _const_spec((1, GLA_KW)),
        ] + w_specs,
        out_specs=[
            pl.BlockSpec((tm, IN_A), lambda i: (i, 0)),
            pl.BlockSpec((tm, IN_G), lambda i: (i, 0)),
            pl.BlockSpec((tm, GLA_KW), lambda i: (i, 0)),
        ] + w_specs,
        out_shape=[
            jax.ShapeDtypeStruct((n, IN_A), BF16),
            jax.ShapeDtypeStruct((n, IN_G), BF16),
            jax.ShapeDtypeStruct((n, GLA_KW), F32),
            jax.ShapeDtypeStruct(w_up.shape, BF16),
            jax.ShapeDtypeStruct(w_down.shape, BF16),
            jax.ShapeDtypeStruct(w_out.shape, BF16),
        ],
        compiler_params=pltpu.CompilerParams(
            dimension_semantics=("arbitrary",), vmem_limit_bytes=VMEM_LIMIT),
        name="in_proj",
    )(x2, ln1_w, w_all, wg2, bg, w_up, w_down, w_out)


def _diff_attn_kernel(lam_init, lq1_ref, lk1_ref, lq2_ref, lk2_ref, sw_ref,
                      q_ref, k_ref, v_ref, o_ref,
                      vt_ref, bias_ref, s_ref4, p_ref4):
    assert TQ == TK
    seq = q_ref.shape[1]
    nq = seq // TQ
    nk = seq // TK

    lam = (jnp.exp(jnp.sum(lq1_ref[...] * lk1_ref[...], axis=-1, keepdims=True))
           - jnp.exp(jnp.sum(lq2_ref[...] * lk2_ref[...], axis=-1, keepdims=True))
           + lam_init)

    hd = 2 * DIFF_QK_DIM
    heads = q_ref.shape[2] // hd
    for hh in range(heads):
        for t in range(nk):
            vt_ref[hh, :, t * TK:(t + 1) * TK] = (
                v_ref[0, t * TK:(t + 1) * TK, hh * DIFF_V_DIM:(hh + 1) * DIFF_V_DIM]
                .astype(F32).T.astype(BF16))

    row = lax.broadcasted_iota(jnp.int32, (2 * DIFF_QK_DIM, TQ), 0)
    krow = lax.broadcasted_iota(jnp.int32, (TQ, 2 * TQ), 0)
    qcol = lax.broadcasted_iota(jnp.int32, (TQ, 2 * TQ), 1) % TQ
    bias_ref[...] = jnp.where(krow <= qcol, 0.0, -jnp.inf).astype(F32)

    def fold8(x, op):
        parts = [x[a * 8:(a + 1) * 8, :] for a in range(x.shape[0] // 8)]
        while len(parts) > 1:
            parts = [op(parts[a], parts[a + 1]) for a in range(0, len(parts), 2)]
        return parts[0]

    def scores(w):
        hh, i = divmod(w, nq)
        nkeys = (i + 1) * TQ
        qt = q_ref[0, i * TQ:(i + 1) * TQ, hh * hd:(hh + 1) * hd].astype(F32).T.astype(BF16)
        zero = jnp.zeros_like(qt)
        q12t = jnp.concatenate([jnp.where(row < DIFF_QK_DIM, qt, zero),
                                jnp.where(row >= DIFF_QK_DIM, qt, zero)], axis=1)
        s_ref4[w % 2, 0:nkeys, :] = _dot(k_ref[0, 0:nkeys, hh * hd:(hh + 1) * hd], q12t)

    def softmax(w):
        i = w % nq
        nblk = (i + 1) * TQ // ROW_BLK

        def block(r):
            blk = s_ref4[w % 2, r * ROW_BLK:(r + 1) * ROW_BLK, :]
            d = r * ROW_BLK - i * TQ
            if d >= 0:
                blk = blk + bias_ref[d:d + ROW_BLK, :]
            return blk

        m8 = fold8(block(0), jnp.maximum)
        for r in range(1, nblk):
            m8 = jnp.maximum(m8, fold8(block(r), jnp.maximum))
        m = jnp.max(m8, axis=0, keepdims=True)
        mb = jnp.broadcast_to(m, (ROW_BLK, 2 * TQ))
        l8 = jnp.zeros((8, 2 * TQ), F32)
        for r in range(nblk):
            p = jnp.exp(block(r) - mb)
            l8 = l8 + fold8(p, jnp.add)
            p_ref4[w % 2, r * ROW_BLK:(r + 1) * ROW_BLK, :] = p.astype(BF16)
        return jnp.sum(l8, axis=0, keepdims=True)

    def values(w, l):
        hh, i = divmod(w, nq)
        nkeys = (i + 1) * TQ
        acc = _dot(vt_ref[hh, :, 0:nkeys], p_ref4[w % 2, 0:nkeys, :]) / l
        ot = acc[:, :TQ] - lam * acc[:, TQ:]
        ms = jnp.mean(ot * ot, axis=0, keepdims=True)
        ot = ot * lax.rsqrt(ms + EPS) * sw_ref[...] * (1.0 - lam_init)
        o_ref[0, i * TQ:(i + 1) * TQ, hh * DIFF_V_DIM:(hh + 1) * DIFF_V_DIM] = ot.T.astype(BF16)

    nwork = heads * nq
    scores(0)
    sums = {}
    for w in range(nwork):
        if w > 0:
            values(w - 1, sums.pop(w - 1))
        if w + 1 < nwork:
            scores(w + 1)
        sums[w] = softmax(w)
    values(nwork - 1, sums.pop(nwork - 1))


def _diff_attn(a3, lq1, lk1, lq2, lk2, subln_col, lam_init):
    b, seq, _ = a3.shape
    hd = 2 * DIFF_QK_DIM
    assert hd == DIFF_V_DIM
    hps = ATTN_HEADS_PER_STEP
    groups = DIFF_HEADS // hps
    vec = _const_spec((1, DIFF_QK_DIM))
    return pl.pallas_call(
        functools.partial(_diff_attn_kernel, lam_init),
        grid=(b, groups),
        in_specs=[
            vec, vec, vec, vec,
            _const_spec((DIFF_V_DIM, 1)),
            pl.BlockSpec((1, seq, hps * hd), lambda bi, g: (bi, 0, g)),
            pl.BlockSpec((1, seq, hps * hd), lambda bi, g: (bi, 0, groups + g)),
            pl.BlockSpec((1, seq, hps * DIFF_V_DIM), lambda bi, g: (bi, 0, 2 * groups + g)),
        ],
        out_specs=pl.BlockSpec((1, seq, hps * DIFF_V_DIM), lambda bi, g: (bi, 0, g)),
        out_shape=jax.ShapeDtypeStruct((b, seq, DIFF_WIDTH), BF16),
        scratch_shapes=[
            pltpu.VMEM((hps, DIFF_V_DIM, seq), BF16),
            pltpu.VMEM((TQ, 2 * TQ), F32),
            pltpu.VMEM((2, seq, 2 * TQ), F32),
            pltpu.VMEM((2, seq, 2 * TQ), BF16),
        ],
        compiler_params=pltpu.CompilerParams(
            dimension_semantics=("arbitrary", "arbitrary"), vmem_limit_bytes=VMEM_LIMIT),
        name="diff_attn",
    )(lq1, lk1, lq2, lk2, subln_col, a3, a3, a3)


def _gla_kernel(g_ref, lg_ref, nw_ref, o_ref, qd_ref, keh_ref, dec_ref, oi_ref, st_ref):
    c = GLA_CHUNK
    blk = GLA_BLK
    cpb = blk // c
    seq = g_ref.shape[1]
    kw, vw, dv = GLA_KW, GLA_WIDTH, GLA_V_DIM
    pw = 2 * GLA_K_DIM
    assert pw == LANES

    ri = lax.broadcasted_iota(jnp.int32, (blk, blk), 0)
    ci = lax.broadcasted_iota(jnp.int32, (blk, blk), 1)
    intra = (ri // c == ci // c) & (ri >= ci)
    tri = intra.astype(BF16)
    lane_half = lax.broadcasted_iota(jnp.int32, (blk, pw), 1) // GLA_K_DIM
    row_half = lax.broadcasted_iota(jnp.int32, (pw, blk), 0) // GLA_K_DIM

    def pair(h):
        p = h // 2
        return slice(p * pw, (p + 1) * pw)

    def vcols(h):
        return slice(h * dv, (h + 1) * dv)

    st_ref[...] = jnp.zeros_like(st_ref)

    def span(start, size):
        if isinstance(start, int):
            return slice(start, start + size)
        return pl.ds(pl.multiple_of(start, size), size)

    def prep(bi):
        rows = span(bi * blk, blk)
        lg = lg_ref[0, rows, :]
        hi = lg.astype(BF16)
        lo = (lg - hi.astype(F32)).astype(BF16)
        b = _dot(tri, hi) + _dot(tri, lo)
        tot = jnp.concatenate(
            [jnp.broadcast_to(b[j * c + c - 1:(j + 1) * c, :], (c, kw)) for j in range(cpb)], axis=0)
        g = g_ref[0, rows, :]
        q = g[:, 0:kw].astype(F32)
        k = g[:, kw:2 * kw].astype(F32)
        qd = (q * jnp.exp(b)).astype(BF16)
        kd = k * jnp.exp(-b)
        ke = (k * jnp.exp(tot - b)).astype(BF16)
        qd_ref[rows, :] = qd
        dec = jnp.exp(tot)
        for j in range(cpb):
            dec_ref[bi * cpb + j] = dec[j * c:j * c + 8, :]
        kdt = kd.T.astype(BF16)
        for h in range(GLA_HEADS):
            keh_ref[h, rows, :] = jnp.where(lane_half == h % 2, ke[:, pair(h)], jnp.zeros((), BF16))
            kdt_h = jnp.where(row_half == h % 2, kdt[pair(h), :], jnp.zeros((), BF16))
            a = jnp.where(intra, _dot(qd[:, pair(h)], kdt_h), 0.0).astype(BF16)
            oi_ref[rows, vcols(h)] = _dot(a, g[:, 2 * kw + h * dv:2 * kw + (h + 1) * dv])

    def scan(n):
        rows = span(n * c, c)
        for h in range(GLA_HEADS):
            st = st_ref[h]
            oi_ref[rows, vcols(h)] += _dot_nt(qd_ref[rows, pair(h)], st.astype(BF16))
            v_h = g_ref[0, rows, 2 * kw + h * dv:2 * kw + (h + 1) * dv]
            st_ref[h] = st * dec_ref[n, 0:1, pair(h)] + _dot_tn(v_h, keh_ref[h, rows, :])

    def scan_block(bi):
        for j in range(cpb):
            scan(bi * cpb + j)

    def finish(bi):
        rows = span(bi * blk, blk)
        nw = nw_ref[...]
        for h in range(GLA_HEADS):
            gate = g_ref[0, rows, 2 * kw + vw + h * dv:2 * kw + vw + (h + 1) * dv].astype(F32)
            y = _rms(oi_ref[rows, vcols(h)], nw) * (gate / (1.0 + jnp.exp(-gate)))
            o_ref[0, rows, vcols(h)] = y.astype(BF16)

    nblk = seq // blk
    prep(0)
    for bi in range(nblk):
        if bi + 1 < nblk:
            prep(bi + 1)
        scan_block(bi)
        finish(bi)


def _gla(g3, lg3, norm_w):
    b, seq, gw = g3.shape
    return pl.pallas_call(
        _gla_kernel,
        grid=(b,),
        in_specs=[
            pl.BlockSpec((1, seq, gw), lambda bi: (bi, 0, 0)),
            pl.BlockSpec((1, seq, GLA_KW), lambda bi: (bi, 0, 0)),
            _const_spec((1, GLA_V_DIM)),
        ],
        out_specs=pl.BlockSpec((1, seq, GLA_WIDTH), lambda bi: (bi, 0, 0)),
        out_shape=jax.ShapeDtypeStruct((b, seq, GLA_WIDTH), BF16),
        scratch_shapes=[
            pltpu.VMEM((seq, GLA_KW), BF16),
            pltpu.VMEM((GLA_HEADS, seq, 2 * GLA_K_DIM), BF16),
            pltpu.VMEM((seq // GLA_CHUNK, 8, GLA_KW), F32),
            pltpu.VMEM((seq, GLA_WIDTH), F32),
            pltpu.VMEM((GLA_HEADS, GLA_V_DIM, 2 * GLA_K_DIM), F32),
        ],
        compiler_params=pltpu.CompilerParams(
            dimension_semantics=("arbitrary",), vmem_limit_bytes=VMEM_LIMIT),
        name="gla",
    )(g3, lg3, norm_w)


def _mlp_kernel(tiles_per_seq, x_ref, od_ref, og_ref, wo_ref, ln2_ref, wup_ref, cw_ref, cb_ref,
                wdn_ref, lnf_ref, o_ref, x1_ref, h_ref, u_ref, act_ref, carry_ref):
    tm = x_ref.shape[0]
    halo = 8
    ck = FFN_CHUNK
    nch = D_FF // ck

    @pl.when(pl.program_id(0) % tiles_per_seq == 0)
    def _():
        carry_ref[...] = jnp.zeros_like(carry_ref)

    mix = jnp.concatenate([od_ref[...], og_ref[...]], axis=1)
    x1 = x_ref[...] + _dot(mix, wo_ref[...])
    x1_ref[...] = x1
    h_ref[...] = _rms(x1, ln2_ref[...]).astype(BF16)

    def gate_value(ref, c):
        return jnp.concatenate([ref[:, c * ck:(c + 1) * ck],
                                ref[:, D_FF + c * ck:D_FF + (c + 1) * ck]], axis=1)

    def up(c):
        par = c % 2
        u = _dot(h_ref[...], gate_value(wup_ref, c))
        u_ref[par, 0:halo, :] = carry_ref[c]
        u_ref[par, halo:halo + tm, :] = u
        carry_ref[c] = u[tm - halo:tm, :]

    def conv_act(c):
        par = c % 2
        w = gate_value(cw_ref, c)
        y = (u_ref[par, halo:halo + tm, :] * w[2:3, :]
             + u_ref[par, halo - 1:halo - 1 + tm, :] * w[1:2, :]
             + u_ref[par, halo - 2:halo - 2 + tm, :] * w[0:1, :]
             + gate_value(cb_ref, c))
        gate, val = y[:, :ck], y[:, ck:]
        act_ref[:, c * ck:(c + 1) * ck] = (gate / (1.0 + jnp.exp(-gate)) * val).astype(BF16)

    def down(c0, c1):
        return _dot(act_ref[:, c0 * ck:c1 * ck], wdn_ref[c0 * ck:c1 * ck, :])

    groups = [(g, min(g + DOWN_GROUP, nch)) for g in range(0, nch, DOWN_GROUP)]
    up(0)
    for c in range(nch):
        if c + 1 < nch:
            up(c + 1)
        for g0, g1 in groups[:-1]:
            if g1 == c:
                x1_ref[...] += down(g0, g1)
        conv_act(c)
    o_ref[...] = _rms(x1_ref[...] + down(*groups[-1]), lnf_ref[...])


def _mlp(x2, od, og, wo, ln2_w, wup, cw, cb, wdn, lnf_w, seq):
    n = x2.shape[0]
    tm = TM_FFN
    nch = D_FF // FFN_CHUNK
    tok = lambda w: pl.BlockSpec((tm, w), lambda i: (i, 0))
    return pl.pallas_call(
        functools.partial(_mlp_kernel, seq // tm),
        grid=(n // tm,),
        in_specs=[tok(D_MODEL), tok(DIFF_WIDTH), tok(GLA_WIDTH),
                  _const_spec(wo.shape), _const_spec((1, D_MODEL)),
                  _const_spec(wup.shape), _const_spec(cw.shape), _const_spec(cb.shape),
                  _const_spec(wdn.shape), _const_spec((1, D_MODEL))],
        out_specs=tok(D_MODEL),
        out_shape=jax.ShapeDtypeStruct((n, D_MODEL), F32),
        scratch_shapes=[
            pltpu.VMEM((tm, D_MODEL), F32),
            pltpu.VMEM((tm, D_MODEL), BF16),
            pltpu.VMEM((2, tm + 8, 2 * FFN_CHUNK), F32),
            pltpu.VMEM((tm, D_FF), BF16),
            pltpu.VMEM((nch, 8, 2 * FFN_CHUNK), F32),
        ],
        compiler_params=pltpu.CompilerParams(
            dimension_semantics=("arbitrary",), vmem_limit_bytes=VMEM_LIMIT),
        name="mlp",
    )(x2, od, og, wo, ln2_w, wup, cw, cb, wdn, lnf_w)


def kernel(x, ln1_w, w_in, diff_lq1, diff_lk1, diff_lq2, diff_lk2, diff_subln_w,
           gla_wg2, gla_bg, gla_norm_w, w_out, ln2_w, w_up, conv_w, conv_b,
           w_down, lnf_w):
    b, seq, d = x.shape
    n = b * seq
    depth = w_in.shape[0]
    assert depth == 1, depth
    x2 = x.reshape(n, d)
    qk_w = DIFF_HEADS * 2 * DIFF_QK_DIM
    c0 = 2 * qk_w + DIFF_WIDTH
    c1 = c0 + 2 * GLA_KW + GLA_WIDTH
    c2 = c1 + GLA_GATE_RANK
    for l in range(depth):
        lam_init = 0.8 - 0.6 * math.exp(-0.3 * l)
        wt = jnp.swapaxes(w_in[l], 0, 1)
        scale = jnp.ones((c0 + IN_G + IN_R, 1), F32)
        scale = scale.at[:qk_w].set(DIFF_QK_DIM ** -0.5).at[c0:c0 + GLA_KW].set(GLA_K_DIM ** -0.5)
        w_all = (jnp.concatenate(
            [wt[:c1], wt[c2:], wt[c1:c2], jnp.zeros((IN_R - GLA_GATE_RANK, d), F32)], axis=0)
            * scale).astype(BF16)
        wg2 = jnp.pad(gla_wg2[l], ((0, IN_R - GLA_GATE_RANK), (0, 0))).astype(BF16)

        a, g, lg, w_up16, w_down16, w_out16 = _in_proj(
            x2, ln1_w[l][None, :], w_all, wg2, gla_bg[l][None, :], w_up[l], w_down[l], w_out[l])
        o_diff = _diff_attn(a.reshape(b, seq, -1),
                            diff_lq1[l][None, :], diff_lk1[l][None, :],
                            diff_lq2[l][None, :], diff_lk2[l][None, :],
                            diff_subln_w[l][:, None], lam_init)
        o_gla = _gla(g.reshape(b, seq, -1), lg.reshape(b, seq, -1), gla_norm_w[l][None, :])

        x2 = _mlp(x2, o_diff.reshape(n, -1), o_gla.reshape(n, -1), w_out16,
                  ln2_w[l][None, :], w_up16, conv_w[l], conv_b[l][None, :],
                  w_down16, lnf_w[None, :], seq)
    return x2.reshape(b, seq, d)
```

```python
import functools
import math

import jax
import jax.numpy as jnp
from jax import lax
from jax.experimental import pallas as pl
from jax.experimental.pallas import tpu as pltpu

F32 = jnp.float32
BF16 = jnp.bfloat16

D_MODEL = 1024
DIFF_HEADS = 4
DIFF_QK_DIM = 64
DIFF_V_DIM = 128
DIFF_WIDTH = DIFF_HEADS * DIFF_V_DIM
GLA_HEADS = 4
GLA_K_DIM = 64
GLA_V_DIM = 128
GLA_KW = GLA_HEADS * GLA_K_DIM
GLA_WIDTH = GLA_HEADS * GLA_V_DIM
GLA_GATE_RANK = 16
GLA_TAU = 16.0
GLA_CHUNK = 64
GLA_BLK = 256
D_FF = 2816
EPS = 1e-6

LANES = 128
VMEM_LIMIT = 62 * 1024 * 1024

TM_PROJ = 1024
TM_FFN = 512
FFN_CHUNK = 1408
DOWN_GROUP = 1
assert D_FF % FFN_CHUNK == 0 and FFN_CHUNK % LANES == 0
TQ = 256
TK = 256
ROW_BLK = 64
ATTN_HEADS_PER_STEP = 2


def _dot(a, b):
    return jnp.dot(a, b, preferred_element_type=F32)


def _dot_nt(a, b):
    return lax.dot_general(a, b, (((1,), (1,)), ((), ())), preferred_element_type=F32)


def _dot_tn(a, b):
    return lax.dot_general(a, b, (((0,), (0,)), ((), ())), preferred_element_type=F32)


def _rms(x, w):
    ms = jnp.mean(x * x, axis=-1, keepdims=True)
    return x * lax.rsqrt(ms + EPS) * w


def _const_spec(shape):
    nd = len(shape)
    return pl.BlockSpec(shape, lambda *_: (0,) * nd, pipeline_mode=pl.Buffered(1))


IN_A = 2 * DIFF_HEADS * 2 * DIFF_QK_DIM + DIFF_WIDTH
IN_G = 2 * GLA_KW + 2 * GLA_WIDTH
IN_R = LANES


def _in_proj_kernel(x_ref, ln_ref, w_ref, wg2_ref, bg_ref, wup_ref, wdn_ref, wo_ref,
                    a_ref, g_ref, lg_ref, wup16_ref, wdn16_ref, wo16_ref):
    h = _rms(x_ref[...], ln_ref[...]).astype(BF16)
    a_ref[...] = _dot_nt(h, w_ref[0:IN_A, :]).astype(BF16)
    gr = _dot_nt(h, w_ref[IN_A:IN_A + IN_G + IN_R, :])
    g_ref[...] = gr[:, 0:IN_G].astype(BF16)
    r = gr[:, IN_G:IN_G + IN_R].astype(BF16)
    logits = _dot(r, wg2_ref[...]) + bg_ref[...]
    ls = jnp.minimum(logits, 0.0) - jnp.log(1.0 + jnp.exp(-jnp.abs(logits)))
    lg_ref[...] = ls * (1.0 / GLA_TAU)
    wup16_ref[...] = wup_ref[...].astype(BF16)
    wdn16_ref[...] = wdn_ref[...].astype(BF16)
    wo16_ref[...] = wo_ref[...].astype(BF16)


def _in_proj(x2, ln1_w, w_all, wg2, bg, w_up, w_down, w_out):
    n = x2.shape[0]
    tm = TM_PROJ
    steps = n // tm
    up_rows = w_up.shape[0] // steps
    out_rows = w_out.shape[0] // steps
    dn_rows = 2 * w_down.shape[0] // steps
    assert up_rows * steps == w_up.shape[0] and out_rows * steps == w_out.shape[0]
    assert dn_rows * steps == 2 * w_down.shape[0]
    assert up_rows % 16 == 0 and out_rows % 16 == 0 and dn_rows % 16 == 0
    slab = lambda rows, w, every: pl.BlockSpec((rows, w), lambda i: (i // every, 0))
    w_specs = [slab(up_rows, w_up.shape[1], 1), slab(dn_rows, w_down.shape[1], 2),
               slab(out_rows, w_out.shape[1], 1)]
    return pl.pallas_call(
        _in_proj_kernel,
        grid=(steps,),
        in_specs=[
            pl.BlockSpec((tm, D_MODEL), lambda i: (i, 0)),
            _const_spec((1, D_MODEL)),
            _const_spec(w_all.shape),
            _const_spec(wg2.shape),
            _const_spec((1, GLA_KW)),
        ] + w_specs,
        out_specs=[
            pl.BlockSpec((tm, IN_A), lambda i: (i, 0)),
            pl.BlockSpec((tm, IN_G), lambda i: (i, 0)),
            pl.BlockSpec((tm, GLA_KW), lambda i: (i, 0)),
        ] + w_specs,
        out_shape=[
            jax.ShapeDtypeStruct((n, IN_A), BF16),
            jax.ShapeDtypeStruct((n, IN_G), BF16),
            jax.ShapeDtypeStruct((n, GLA_KW), F32),
            jax.ShapeDtypeStruct(w_up.shape, BF16),
            jax.ShapeDtypeStruct(w_down.shape, BF16),
            jax.ShapeDtypeStruct(w_out.shape, BF16),
        ],
        compiler_params=pltpu.CompilerParams(
            dimension_semantics=("arbitrary",), vmem_limit_bytes=VMEM_LIMIT),
        name="in_proj",
    )(x2, ln1_w, w_all, wg2, bg, w_up, w_down, w_out)


def _diff_attn_kernel(lam_init, lq1_ref, lk1_ref, lq2_ref, lk2_ref, sw_ref,
                      q_ref, k_ref, v_ref, o_ref,
                      vt_ref, bias_ref, s_ref4, p_ref4):
    assert TQ == TK
    seq = q_ref.shape[1]
    nq = seq // TQ
    nk = seq // TK

    lam = (jnp.exp(jnp.sum(lq1_ref[...] * lk1_ref[...], axis=-1, keepdims=True))
           - jnp.exp(jnp.sum(lq2_ref[...] * lk2_ref[...], axis=-1, keepdims=True))
           + lam_init)

    hd = 2 * DIFF_QK_DIM
    heads = q_ref.shape[2] // hd
    for hh in range(heads):
        for t in range(nk):
            vt_ref[hh, :, t * TK:(t + 1) * TK] = (
                v_ref[0, t * TK:(t + 1) * TK, hh * DIFF_V_DIM:(hh + 1) * DIFF_V_DIM]
                .astype(F32).T.astype(BF16))

    row = lax.broadcasted_iota(jnp.int32, (2 * DIFF_QK_DIM, TQ), 0)
    krow = lax.broadcasted_iota(jnp.int32, (TQ, 2 * TQ), 0)
    qcol = lax.broadcasted_iota(jnp.int32, (TQ, 2 * TQ), 1) % TQ
    bias_ref[...] = jnp.where(krow <= qcol, 0.0, -jnp.inf).astype(F32)

    def fold8(x, op):
        parts = [x[a * 8:(a + 1) * 8, :] for a in range(x.shape[0] // 8)]
        while len(parts) > 1:
            parts = [op(parts[a], parts[a + 1]) for a in range(0, len(parts), 2)]
        return parts[0]

    def scores(w):
        hh, i = divmod(w, nq)
        nkeys = (i + 1) * TQ
        qt = q_ref[0, i * TQ:(i + 1) * TQ, hh * hd:(hh + 1) * hd].astype(F32).T.astype(BF16)
        zero = jnp.zeros_like(qt)
        q12t = jnp.concatenate([jnp.where(row < DIFF_QK_DIM, qt, zero),
                                jnp.where(row >= DIFF_QK_DIM, qt, zero)], axis=1)
        s_ref4[w % 2, 0:nkeys, :] = _dot(k_ref[0, 0:nkeys, hh * hd:(hh + 1) * hd], q12t)

    def softmax(w):
        i = w % nq
        nblk = (i + 1) * TQ // ROW_BLK

        def block(r):
            blk = s_ref4[w % 2, r * ROW_BLK:(r + 1) * ROW_BLK, :]
            d = r * ROW_BLK - i * TQ
            if d >= 0:
                blk = blk + bias_ref[d:d + ROW_BLK, :]
            return blk

        m8 = fold8(block(0), jnp.maximum)
        for r in range(1, nblk):
            m8 = jnp.maximum(m8, fold8(block(r), jnp.maximum))
        m = jnp.max(m8, axis=0, keepdims=True)
        mb = jnp.broadcast_to(m, (ROW_BLK, 2 * TQ))
        l8 = jnp.zeros((8, 2 * TQ), F32)
        for r in range(nblk):
            p = jnp.exp(block(r) - mb)
            l8 = l8 + fold8(p, jnp.add)
            p_ref4[w % 2, r * ROW_BLK:(r + 1) * ROW_BLK, :] = p.astype(BF16)
        return jnp.sum(l8, axis=0, keepdims=True)

    def values(w, l):
        hh, i = divmod(w, nq)
        nkeys = (i + 1) * TQ
        acc = _dot(vt_ref[hh, :, 0:nkeys], p_ref4[w % 2, 0:nkeys, :]) / l
        ot = acc[:, :TQ] - lam * acc[:, TQ:]
        ms = jnp.mean(ot * ot, axis=0, keepdims=True)
        ot = ot * lax.rsqrt(ms + EPS) * sw_ref[...] * (1.0 - lam_init)
        o_ref[0, i * TQ:(i + 1) * TQ, hh * DIFF_V_DIM:(hh + 1) * DIFF_V_DIM] = ot.T.astype(BF16)

    nwork = heads * nq
    scores(0)
    sums = {}
    for w in range(nwork):
        if w > 0:
            values(w - 1, sums.pop(w - 1))
        if w + 1 < nwork:
            scores(w + 1)
        sums[w] = softmax(w)
    values(nwork - 1, sums.pop(nwork - 1))


def _diff_attn(a3, lq1, lk1, lq2, lk2, subln_col, lam_init):
    b, seq, _ = a3.shape
    hd = 2 * DIFF_QK_DIM
    assert hd == DIFF_V_DIM
    hps = ATTN_HEADS_PER_STEP
    groups = DIFF_HEADS // hps
    vec = _const_spec((1, DIFF_QK_DIM))
    return pl.pallas_call(
        functools.partial(_diff_attn_kernel, lam_init),
        grid=(b, groups),
        in_specs=[
            vec, vec, vec, vec,
            _const_spec((DIFF_V_DIM, 1)),
            pl.BlockSpec((1, seq, hps * hd), lambda bi, g: (bi, 0, g)),
            pl.BlockSpec((1, seq, hps * hd), lambda bi, g: (bi, 0, groups + g)),
            pl.BlockSpec((1, seq, hps * DIFF_V_DIM), lambda bi, g: (bi, 0, 2 * groups + g)),
        ],
        out_specs=pl.BlockSpec((1, seq, hps * DIFF_V_DIM), lambda bi, g: (bi, 0, g)),
        out_shape=jax.ShapeDtypeStruct((b, seq, DIFF_WIDTH), BF16),
        scratch_shapes=[
            pltpu.VMEM((hps, DIFF_V_DIM, seq), BF16),
            pltpu.VMEM((TQ, 2 * TQ), F32),
            pltpu.VMEM((2, seq, 2 * TQ), F32),
            pltpu.VMEM((2, seq, 2 * TQ), BF16),
        ],
        compiler_params=pltpu.CompilerParams(
            dimension_semantics=("arbitrary", "arbitrary"), vmem_limit_bytes=VMEM_LIMIT),
        name="diff_attn",
    )(lq1, lk1, lq2, lk2, subln_col, a3, a3, a3)


def _gla_kernel(g_ref, lg_ref, nw_ref, o_ref, qd_ref, keh_ref, dec_ref, oi_ref, st_ref):
    c = GLA_CHUNK
    blk = GLA_BLK
    cpb = blk // c
    seq = g_ref.shape[1]
    kw, vw, dv = GLA_KW, GLA_WIDTH, GLA_V_DIM
    pw = 2 * GLA_K_DIM
    assert pw == LANES

    ri = lax.broadcasted_iota(jnp.int32, (blk, blk), 0)
    ci = lax.broadcasted_iota(jnp.int32, (blk, blk), 1)
    intra = (ri // c == ci // c) & (ri >= ci)
    tri = intra.astype(BF16)
    cross = (ri // c == ci // c + 1) & ((ri // c) % 2 == 1)
    assert cpb % 2 == 0
    lane_half = lax.broadcasted_iota(jnp.int32, (blk, pw), 1) // GLA_K_DIM
    row_half = lax.broadcasted_iota(jnp.int32, (pw, blk), 0) // GLA_K_DIM

    def pair(h):
        p = h // 2
        return slice(p * pw, (p + 1) * pw)

    def vcols(h):
        return slice(h * dv, (h + 1) * dv)

    st_ref[...] = jnp.zeros_like(st_ref)

    def span(start, size):
        if isinstance(start, int):
            return slice(start, start + size)
        return pl.ds(pl.multiple_of(start, size), size)

    def prep(bi):
        rows = span(bi * blk, blk)
        lg = lg_ref[0, rows, :]
        hi = lg.astype(BF16)
        lo = (lg - hi.astype(F32)).astype(BF16)
        b = _dot(tri, hi) + _dot(tri, lo)
        bl = [b[j * c + c - 1:(j + 1) * c, :] for j in range(cpb)]
        rep = lambda t: jnp.broadcast_to(t, (c, kw))
        zero = jnp.zeros((c, kw), F32)
        tot = jnp.concatenate([rep(t) for t in bl], axis=0)
        tot_prev = jnp.concatenate([rep(bl[j - 1]) if j % 2 else zero for j in range(cpb)], axis=0)
        tot_next = jnp.concatenate([zero if j % 2 else rep(bl[j + 1]) for j in range(cpb)], axis=0)
        g = g_ref[0, rows, :]
        q = g[:, 0:kw].astype(F32)
        k = g[:, kw:2 * kw].astype(F32)
        qd = (q * jnp.exp(b)).astype(BF16)
        kd = k * jnp.exp(-b)
        ke = k * jnp.exp(tot - b)
        ke2 = (k * jnp.exp(tot - b + tot_next)).astype(BF16)
        qd_ref[rows, :] = (q * jnp.exp(b + tot_prev)).astype(BF16)
        for m in range(cpb // 2):
            dec_ref[bi * (cpb // 2) + m] = jnp.broadcast_to(jnp.exp(bl[2 * m] + bl[2 * m + 1]), (8, kw))
        kdt = kd.T.astype(BF16)
        ket = ke.T.astype(BF16)
        for h in range(GLA_HEADS):
            keh_ref[h, rows, :] = jnp.where(lane_half == h % 2, ke2[:, pair(h)], jnp.zeros((), BF16))
            kdt_h = jnp.where(row_half == h % 2, kdt[pair(h), :], jnp.zeros((), BF16))
            ket_h = jnp.where(row_half == h % 2, ket[pair(h), :], jnp.zeros((), BF16))
            qh = qd[:, pair(h)]
            a = jnp.where(intra, _dot(qh, kdt_h), jnp.where(cross, _dot(qh, ket_h), 0.0)).astype(BF16)
            oi_ref[rows, vcols(h)] = _dot(a, g[:, 2 * kw + h * dv:2 * kw + (h + 1) * dv])

    def scan(m):
        rows = span(m * 2 * c, 2 * c)
        for h in range(GLA_HEADS):
            st = st_ref[h]
            oi_ref[rows, vcols(h)] += _dot_nt(qd_ref[rows, pair(h)], st.astype(BF16))
            v_h = g_ref[0, rows, 2 * kw + h * dv:2 * kw + (h + 1) * dv]
            st_ref[h] = st * dec_ref[m, 0:1, pair(h)] + _dot_tn(v_h, keh_ref[h, rows, :])

    def scan_block(bi):
        for m in range(cpb // 2):
            scan(bi * (cpb // 2) + m)

    def finish(bi):
        rows = span(bi * blk, blk)
        nw = nw_ref[...]
        for h in range(GLA_HEADS):
            gate = g_ref[0, rows, 2 * kw + vw + h * dv:2 * kw + vw + (h + 1) * dv].astype(F32)
            y = _rms(oi_ref[rows, vcols(h)], nw) * (gate / (1.0 + jnp.exp(-gate)))
            o_ref[0, rows, vcols(h)] = y.astype(BF16)

    nblk = seq // blk
    prep(0)
    for bi in range(nblk):
        if bi + 1 < nblk:
            prep(bi + 1)
        scan_block(bi)
        finish(bi)


def _gla(g3, lg3, norm_w):
    b, seq, gw = g3.shape
    return pl.pallas_call(
        _gla_kernel,
        grid=(b,),
        in_specs=[
            pl.BlockSpec((1, seq, gw), lambda bi: (bi, 0, 0)),
            pl.BlockSpec((1, seq, GLA_KW), lambda bi: (bi, 0, 0)),
            _const_spec((1, GLA_V_DIM)),
        ],
        out_specs=pl.BlockSpec((1, seq, GLA_WIDTH), lambda bi: (bi, 0, 0)),
        out_shape=jax.ShapeDtypeStruct((b, seq, GLA_WIDTH), BF16),
        scratch_shapes=[
            pltpu.VMEM((seq, GLA_KW), BF16),
            pltpu.VMEM((GLA_HEADS, seq, 2 * GLA_K_DIM), BF16),
            pltpu.VMEM((seq // (2 * GLA_CHUNK), 8, GLA_KW), F32),
            pltpu.VMEM((seq, GLA_WIDTH), F32),
            pltpu.VMEM((GLA_HEADS, GLA_V_DIM, 2 * GLA_K_DIM), F32),
        ],
        compiler_params=pltpu.CompilerParams(
            dimension_semantics=("arbitrary",), vmem_limit_bytes=VMEM_LIMIT),
        name="gla",
    )(g3, lg3, norm_w)


def _mlp_kernel(tiles_per_seq, x_ref, od_ref, og_ref, wo_ref, ln2_ref, wup_ref, cw_ref, cb_ref,
                wdn_ref, lnf_ref, o_ref, x1_ref, h_ref, u_ref, act_ref, carry_ref):
    tm = x_ref.shape[0]
    halo = 8
    ck = FFN_CHUNK
    nch = D_FF // ck

    @pl.when(pl.program_id(0) % tiles_per_seq == 0)
    def _():
        carry_ref[...] = jnp.zeros_like(carry_ref)

    mix = jnp.concatenate([od_ref[...], og_ref[...]], axis=1)
    x1 = x_ref[...] + _dot(mix, wo_ref[...])
    x1_ref[...] = x1
    h_ref[...] = _rms(x1, ln2_ref[...]).astype(BF16)

    def gate_value(ref, c):
        return jnp.concatenate([ref[:, c * ck:(c + 1) * ck],
                                ref[:, D_FF + c * ck:D_FF + (c + 1) * ck]], axis=1)

    def up(c):
        par = c % 2
        u = _dot(h_ref[...], gate_value(wup_ref, c))
        u_ref[par, 0:halo, :] = carry_ref[c]
        u_ref[par, halo:halo + tm, :] = u
        carry_ref[c] = u[tm - halo:tm, :]

    def conv_act(c):
        par = c % 2
        w = gate_value(cw_ref, c)
        y = (u_ref[par, halo:halo + tm, :] * w[2:3, :]
             + u_ref[par, halo - 1:halo - 1 + tm, :] * w[1:2, :]
             + u_ref[par, halo - 2:halo - 2 + tm, :] * w[0:1, :]
             + gate_value(cb_ref, c))
        gate, val = y[:, :ck], y[:, ck:]
        act_ref[:, c * ck:(c + 1) * ck] = (gate / (1.0 + jnp.exp(-gate)) * val).astype(BF16)

    def down(c0, c1):
        return _dot(act_ref[:, c0 * ck:c1 * ck], wdn_ref[c0 * ck:c1 * ck, :])

    groups = [(g, min(g + DOWN_GROUP, nch)) for g in range(0, nch, DOWN_GROUP)]
    up(0)
    for c in range(nch):
        if c + 1 < nch:
            up(c + 1)
        for g0, g1 in groups[:-1]:
            if g1 == c:
                x1_ref[...] += down(g0, g1)
        conv_act(c)
    o_ref[...] = _rms(x1_ref[...] + down(*groups[-1]), lnf_ref[...])


def _mlp(x2, od, og, wo, ln2_w, wup, cw, cb, wdn, lnf_w, seq):
    n = x2.shape[0]
    tm = TM_FFN
    nch = D_FF // FFN_CHUNK
    tok = lambda w: pl.BlockSpec((tm, w), lambda i: (i, 0))
    return pl.pallas_call(
        functools.partial(_mlp_kernel, seq // tm),
        grid=(n // tm,),
        in_specs=[tok(D_MODEL), tok(DIFF_WIDTH), tok(GLA_WIDTH),
                  _const_spec(wo.shape), _const_spec((1, D_MODEL)),
                  _const_spec(wup.shape), _const_spec(cw.shape), _const_spec(cb.shape),
                  _const_spec(wdn.shape), _const_spec((1, D_MODEL))],
        out_specs=tok(D_MODEL),
        out_shape=jax.ShapeDtypeStruct((n, D_MODEL), F32),
        scratch_shapes=[
            pltpu.VMEM((tm, D_MODEL), F32),
            pltpu.VMEM((tm, D_MODEL), BF16),
            pltpu.VMEM((2, tm + 8, 2 * FFN_CHUNK), F32),
            pltpu.VMEM((tm, D_FF), BF16),
            pltpu.VMEM((nch, 8, 2 * FFN_CHUNK), F32),
        ],
        compiler_params=pltpu.CompilerParams(
            dimension_semantics=("arbitrary",), vmem_limit_bytes=VMEM_LIMIT),
        name="mlp",
    )(x2, od, og, wo, ln2_w, wup, cw, cb, wdn, lnf_w)


def kernel(x, ln1_w, w_in, diff_lq1, diff_lk1, diff_lq2, diff_lk2, diff_subln_w,
           gla_wg2, gla_bg, gla_norm_w, w_out, ln2_w, w_up, conv_w, conv_b,
           w_down, lnf_w):
    b, seq, d = x.shape
    n = b * seq
    depth = w_in.shape[0]
    assert depth == 1, depth
    x2 = x.reshape(n, d)
    qk_w = DIFF_HEADS * 2 * DIFF_QK_DIM
    c0 = 2 * qk_w + DIFF_WIDTH
    c1 = c0 + 2 * GLA_KW + GLA_WIDTH
    c2 = c1 + GLA_GATE_RANK
    for l in range(depth):
        lam_init = 0.8 - 0.6 * math.exp(-0.3 * l)
        wt = jnp.swapaxes(w_in[l], 0, 1)
        scale = jnp.ones((c0 + IN_G + IN_R, 1), F32)
        scale = scale.at[:qk_w].set(DIFF_QK_DIM ** -0.5).at[c0:c0 + GLA_KW].set(GLA_K_DIM ** -0.5)
        w_all = (jnp.concatenate(
            [wt[:c1], wt[c2:], wt[c1:c2], jnp.zeros((IN_R - GLA_GATE_RANK, d), F32)], axis=0)
            * scale).astype(BF16)
        wg2 = jnp.pad(gla_wg2[l], ((0, IN_R - GLA_GATE_RANK), (0, 0))).astype(BF16)

        a, g, lg, w_up16, w_down16, w_out16 = _in_proj(
            x2, ln1_w[l][None, :], w_all, wg2, gla_bg[l][None, :], w_up[l], w_down[l], w_out[l])
        o_diff = _diff_attn(a.reshape(b, seq, -1),
                            diff_lq1[l][None, :], diff_lk1[l][None, :],
                            diff_lq2[l][None, :], diff_lk2[l][None, :],
                            diff_subln_w[l][:, None], lam_init)
        o_gla = _gla(g.reshape(b, seq, -1), lg.reshape(b, seq, -1), gla_norm_w[l][None, :])

        x2 = _mlp(x2, o_diff.reshape(n, -1), o_gla.reshape(n, -1), w_out16,
                  ln2_w[l][None, :], w_up16, conv_w[l], conv_b[l][None, :],
                  w_down16, lnf_w[None, :], seq)
    return x2.reshape(b, seq, d)
```

```python
import functools
import math

import jax
import jax.numpy as jnp
from jax import lax
from jax.experimental import pallas as pl
from jax.experimental.pallas import tpu as pltpu

F32 = jnp.float32
BF16 = jnp.bfloat16

D_MODEL = 1024
DIFF_HEADS = 4
DIFF_QK_DIM = 64
DIFF_V_DIM = 128
DIFF_WIDTH = DIFF_HEADS * DIFF_V_DIM
GLA_HEADS = 4
GLA_K_DIM = 64
GLA_V_DIM = 128
GLA_KW = GLA_HEADS * GLA_K_DIM
GLA_WIDTH = GLA_HEADS * GLA_V_DIM
GLA_GATE_RANK = 16
GLA_TAU = 16.0
GLA_CHUNK = 64
GLA_BLK = 256
D_FF = 2816
EPS = 1e-6

LANES = 128
VMEM_LIMIT = 62 * 1024 * 1024

TM_PROJ = 1024
TM_FFN = 512
FFN_CHUNK = 1408
DOWN_GROUP = 1
assert D_FF % FFN_CHUNK == 0 and FFN_CHUNK % LANES == 0
TQ = 256
TK = 256
ROW_BLK = 64
ATTN_HEADS_PER_STEP = 2


def _dot(a, b):
    return jnp.dot(a, b, preferred_element_type=F32)


def _dot_nt(a, b):
    return lax.dot_general(a, b, (((1,), (1,)), ((), ())), preferred_element_type=F32)


def _dot_tn(a, b):
    return lax.dot_general(a, b, (((0,), (0,)), ((), ())), preferred_element_type=F32)


def _rms(x, w):
    ms = jnp.mean(x * x, axis=-1, keepdims=True)
    return x * lax.rsqrt(ms + EPS) * w


def _const_spec(shape):
    nd = len(shape)
    return pl.BlockSpec(shape, lambda *_: (0,) * nd, pipeline_mode=pl.Buffered(1))


IN_A = 2 * DIFF_HEADS * 2 * DIFF_QK_DIM + DIFF_WIDTH
IN_G = 2 * GLA_KW + 2 * GLA_WIDTH
IN_R = LANES


def _in_proj_kernel(x_ref, ln_ref, w_ref, wg2_ref, bg_ref, wup_ref, wdn_ref, wo_ref,
                    a_ref, g_ref, lg_ref, wup16_ref, wdn16_ref, wo16_ref):
    h = _rms(x_ref[...], ln_ref[...]).astype(BF16)
    a_ref[...] = _dot_nt(h, w_ref[0:IN_A, :]).astype(BF16)
    gr = _dot_nt(h, w_ref[IN_A:IN_A + IN_G + IN_R, :])
    g_ref[...] = gr[:, 0:IN_G].astype(BF16)
    r = gr[:, IN_G:IN_G + IN_R].astype(BF16)
    logits = _dot(r, wg2_ref[...]) + bg_ref[...]
    ls = jnp.minimum(logits, 0.0) - jnp.log(1.0 + jnp.exp(-jnp.abs(logits)))
    lg_ref[...] = ls * (1.0 / GLA_TAU)
    wup16_ref[...] = wup_ref[...].astype(BF16)
    wdn16_ref[...] = wdn_ref[...].astype(BF16)
    wo16_ref[...] = wo_ref[...].astype(BF16)


def _in_proj(x2, ln1_w, w_all, wg2, bg, w_up, w_down, w_out):
    n = x2.shape[0]
    tm = TM_PROJ
    steps = n // tm
    up_rows = w_up.shape[0] // steps
    out_rows = w_out.shape[0] // steps
    dn_rows = 2 * w_down.shape[0] // steps
    assert up_rows * steps == w_up.shape[0] and out_rows * steps == w_out.shape[0]
    assert dn_rows * steps == 2 * w_down.shape[0]
    assert up_rows % 16 == 0 and out_rows % 16 == 0 and dn_rows % 16 == 0
    slab = lambda rows, w, every: pl.BlockSpec((rows, w), lambda i: (i // every, 0))
    w_specs = [slab(up_rows, w_up.shape[1], 1), slab(dn_rows, w_down.shape[1], 2),
               slab(out_rows, w_out.shape[1], 1)]
    return pl.pallas_call(
        _in_proj_kernel,
        grid=(steps,),
        in_specs=[
            pl.BlockSpec((tm, D_MODEL), lambda i: (i, 0)),
            _const_spec((1, D_MODEL)),
            _const_spec(w_all.shape),
            _const_spec(wg2.shape),
            _const_spec((1, GLA_KW)),
        ] + w_specs,
        out_specs=[
            pl.BlockSpec((tm, IN_A), lambda i: (i, 0)),
            pl.BlockSpec((tm, IN_G), lambda i: (i, 0)),
            pl.BlockSpec((tm, GLA_KW), lambda i: (i, 0)),
        ] + w_specs,
        out_shape=[
            jax.ShapeDtypeStruct((n, IN_A), BF16),
            jax.ShapeDtypeStruct((n, IN_G), BF16),
            jax.ShapeDtypeStruct((n, GLA_KW), F32),
            jax.ShapeDtypeStruct(w_up.shape, BF16),
            jax.ShapeDtypeStruct(w_down.shape, BF16),
            jax.ShapeDtypeStruct(w_out.shape, BF16),
        ],
        compiler_params=pltpu.CompilerParams(
            dimension_semantics=("arbitrary",), vmem_limit_bytes=VMEM_LIMIT),
        name="in_proj",
    )(x2, ln1_w, w_all, wg2, bg, w_up, w_down, w_out)


def _diff_attn_kernel(lam_init, lq1_ref, lk1_ref, lq2_ref, lk2_ref, sw_ref,
                      q_ref, k_ref, v_ref, o_ref,
                      vt_ref, bias_ref, s_ref4, p_ref4):
    assert TQ == TK
    seq = q_ref.shape[1]
    nq = seq // TQ
    nk = seq // TK

    lam = (jnp.exp(jnp.sum(lq1_ref[...] * lk1_ref[...], axis=-1, keepdims=True))
           - jnp.exp(jnp.sum(lq2_ref[...] * lk2_ref[...], axis=-1, keepdims=True))
           + lam_init)

    hd = 2 * DIFF_QK_DIM
    heads = q_ref.shape[2] // hd
    for hh in range(heads):
        for t in range(nk):
            vt_ref[hh, :, t * TK:(t + 1) * TK] = (
                v_ref[0, t * TK:(t + 1) * TK, hh * DIFF_V_DIM:(hh + 1) * DIFF_V_DIM]
                .astype(F32).T.astype(BF16))

    row = lax.broadcasted_iota(jnp.int32, (2 * DIFF_QK_DIM, TQ), 0)
    krow = lax.broadcasted_iota(jnp.int32, (TQ, 2 * TQ), 0)
    qcol = lax.broadcasted_iota(jnp.int32, (TQ, 2 * TQ), 1) % TQ
    bias_ref[...] = jnp.where(krow <= qcol, 0.0, -jnp.inf).astype(F32)

    def fold8(x, op):
        parts = [x[a * 8:(a + 1) * 8, :] for a in range(x.shape[0] // 8)]
        while len(parts) > 1:
            parts = [op(parts[a], parts[a + 1]) for a in range(0, len(parts), 2)]
        return parts[0]

    def scores(w):
        hh, i = divmod(w, nq)
        nkeys = (i + 1) * TQ
        qt = q_ref[0, i * TQ:(i + 1) * TQ, hh * hd:(hh + 1) * hd].astype(F32).T.astype(BF16)
        zero = jnp.zeros_like(qt)
        q12t = jnp.concatenate([jnp.where(row < DIFF_QK_DIM, qt, zero),
                                jnp.where(row >= DIFF_QK_DIM, qt, zero)], axis=1)
        s_ref4[w % 2, 0:nkeys, :] = _dot(k_ref[0, 0:nkeys, hh * hd:(hh + 1) * hd], q12t)

    def softmax(w):
        i = w % nq
        nblk = (i + 1) * TQ // ROW_BLK

        def block(r):
            blk = s_ref4[w % 2, r * ROW_BLK:(r + 1) * ROW_BLK, :]
            d = r * ROW_BLK - i * TQ
            if d >= 0:
                blk = blk + bias_ref[d:d + ROW_BLK, :]
            return blk

        m8 = fold8(block(0), jnp.maximum)
        for r in range(1, nblk):
            m8 = jnp.maximum(m8, fold8(block(r), jnp.maximum))
        m = jnp.max(m8, axis=0, keepdims=True)
        mb = jnp.broadcast_to(m, (ROW_BLK, 2 * TQ))
        l8 = jnp.zeros((8, 2 * TQ), F32)
        for r in range(nblk):
            p = jnp.exp(block(r) - mb)
            l8 = l8 + fold8(p, jnp.add)
            p_ref4[w % 2, r * ROW_BLK:(r + 1) * ROW_BLK, :] = p.astype(BF16)
        return jnp.sum(l8, axis=0, keepdims=True)

    def values(w, l):
        hh, i = divmod(w, nq)
        nkeys = (i + 1) * TQ
        acc = _dot(vt_ref[hh, :, 0:nkeys], p_ref4[w % 2, 0:nkeys, :]) / l
        ot = acc[:, :TQ] - lam * acc[:, TQ:]
        ms = jnp.mean(ot * ot, axis=0, keepdims=True)
        ot = ot * lax.rsqrt(ms + EPS) * sw_ref[...] * (1.0 - lam_init)
        o_ref[0, i * TQ:(i + 1) * TQ, hh * DIFF_V_DIM:(hh + 1) * DIFF_V_DIM] = ot.T.astype(BF16)

    nwork = heads * nq
    scores(0)
    sums = {}
    for w in range(nwork):
        if w > 0:
            values(w - 1, sums.pop(w - 1))
        if w + 1 < nwork:
            scores(w + 1)
        sums[w] = softmax(w)
    values(nwork - 1, sums.pop(nwork - 1))


def _diff_attn(a3, lq1, lk1, lq2, lk2, subln_col, lam_init):
    b, seq, _ = a3.shape
    hd = 2 * DIFF_QK_DIM
    assert hd == DIFF_V_DIM
    hps = ATTN_HEADS_PER_STEP
    groups = DIFF_HEADS // hps
    vec = _const_spec((1, DIFF_QK_DIM))
    return pl.pallas_call(
        functools.partial(_diff_attn_kernel, lam_init),
        grid=(b, groups),
        in_specs=[
            vec, vec, vec, vec,
            _const_spec((DIFF_V_DIM, 1)),
            pl.BlockSpec((1, seq, hps * hd), lambda bi, g: (bi, 0, g)),
            pl.BlockSpec((1, seq, hps * hd), lambda bi, g: (bi, 0, groups + g)),
            pl.BlockSpec((1, seq, hps * DIFF_V_DIM), lambda bi, g: (bi, 0, 2 * groups + g)),
        ],
        out_specs=pl.BlockSpec((1, seq, hps * DIFF_V_DIM), lambda bi, g: (bi, 0, g)),
        out_shape=jax.ShapeDtypeStruct((b, seq, DIFF_WIDTH), BF16),
        scratch_shapes=[
            pltpu.VMEM((hps, DIFF_V_DIM, seq), BF16),
            pltpu.VMEM((TQ, 2 * TQ), F32),
            pltpu.VMEM((2, seq, 2 * TQ), F32),
            pltpu.VMEM((2, seq, 2 * TQ), BF16),
        ],
        compiler_params=pltpu.CompilerParams(
            dimension_semantics=("arbitrary", "arbitrary"), vmem_limit_bytes=VMEM_LIMIT),
        name="diff_attn",
    )(lq1, lk1, lq2, lk2, subln_col, a3, a3, a3)


def _gla_kernel(g_ref, lg_ref, nw_ref, o_ref, qd_ref, ke_ref, dec_ref, oi_ref, st_ref):
    c = GLA_CHUNK
    blk = GLA_BLK
    cpb = blk // c
    seq = g_ref.shape[1]
    kw, vw, dv = GLA_KW, GLA_WIDTH, GLA_V_DIM
    pw = 2 * GLA_K_DIM
    assert pw == LANES

    ri = lax.broadcasted_iota(jnp.int32, (blk, blk), 0)
    ci = lax.broadcasted_iota(jnp.int32, (blk, blk), 1)
    intra = (ri // c == ci // c) & (ri >= ci)
    tri = intra.astype(BF16)
    row_half = lax.broadcasted_iota(jnp.int32, (pw, blk), 0) // GLA_K_DIM
    st_diag = (lax.broadcasted_iota(jnp.int32, (2 * dv, pw), 0) // dv
               == lax.broadcasted_iota(jnp.int32, (2 * dv, pw), 1) // GLA_K_DIM)

    def pair(h):
        p = h // 2
        return slice(p * pw, (p + 1) * pw)

    def vcols(h):
        return slice(h * dv, (h + 1) * dv)

    st_ref[...] = jnp.zeros_like(st_ref)

    def span(start, size):
        if isinstance(start, int):
            return slice(start, start + size)
        return pl.ds(pl.multiple_of(start, size), size)

    def prep(bi):
        rows = span(bi * blk, blk)
        lg = lg_ref[0, rows, :]
        hi = lg.astype(BF16)
        lo = (lg - hi.astype(F32)).astype(BF16)
        b = _dot(tri, hi) + _dot(tri, lo)
        tot = jnp.concatenate(
            [jnp.broadcast_to(b[j * c + c - 1:(j + 1) * c, :], (c, kw)) for j in range(cpb)], axis=0)
        g = g_ref[0, rows, :]
        q = g[:, 0:kw].astype(F32)
        k = g[:, kw:2 * kw].astype(F32)
        qd = (q * jnp.exp(b)).astype(BF16)
        kd = k * jnp.exp(-b)
        ke = (k * jnp.exp(tot - b)).astype(BF16)
        qd_ref[rows, :] = qd
        ke_ref[rows, :] = ke
        dec = jnp.exp(tot)
        for j in range(cpb):
            dec_ref[bi * cpb + j] = dec[j * c:j * c + 8, :]
        kdt = kd.T.astype(BF16)
        for h in range(GLA_HEADS):
            kdt_h = jnp.where(row_half == h % 2, kdt[pair(h), :], jnp.zeros((), BF16))
            a = jnp.where(intra, _dot(qd[:, pair(h)], kdt_h), 0.0).astype(BF16)
            oi_ref[rows, vcols(h)] = _dot(a, g[:, 2 * kw + h * dv:2 * kw + (h + 1) * dv])

    def scan(n):
        rows = span(n * c, c)
        for p in range(GLA_HEADS // 2):
            lanes = slice(p * pw, (p + 1) * pw)
            vc = slice(2 * p * dv, 2 * (p + 1) * dv)
            st = st_ref[p]
            oi_ref[rows, vc] += _dot_nt(qd_ref[rows, lanes], st.astype(BF16))
            v_p = g_ref[0, rows, 2 * kw + 2 * p * dv:2 * kw + 2 * (p + 1) * dv]
            cs = jnp.where(st_diag, _dot_tn(v_p, ke_ref[rows, lanes]), 0.0)
            st_ref[p] = st * dec_ref[n, 0:1, lanes] + cs

    def scan_block(bi):
        for j in range(cpb):
            scan(bi * cpb + j)

    def finish(bi):
        rows = span(bi * blk, blk)
        nw = nw_ref[...]
        for h in range(GLA_HEADS):
            gate = g_ref[0, rows, 2 * kw + vw + h * dv:2 * kw + vw + (h + 1) * dv].astype(F32)
            y = _rms(oi_ref[rows, vcols(h)], nw) * (gate / (1.0 + jnp.exp(-gate)))
            o_ref[0, rows, vcols(h)] = y.astype(BF16)

    nblk = seq // blk
    prep(0)
    for bi in range(nblk):
        if bi + 1 < nblk:
            prep(bi + 1)
        scan_block(bi)
        finish(bi)


def _gla(g3, lg3, norm_w):
    b, seq, gw = g3.shape
    return pl.pallas_call(
        _gla_kernel,
        grid=(b,),
        in_specs=[
            pl.BlockSpec((1, seq, gw), lambda bi: (bi, 0, 0)),
            pl.BlockSpec((1, seq, GLA_KW), lambda bi: (bi, 0, 0)),
            _const_spec((1, GLA_V_DIM)),
        ],
        out_specs=pl.BlockSpec((1, seq, GLA_WIDTH), lambda bi: (bi, 0, 0)),
        out_shape=jax.ShapeDtypeStruct((b, seq, GLA_WIDTH), BF16),
        scratch_shapes=[
            pltpu.VMEM((seq, GLA_KW), BF16),
            pltpu.VMEM((seq, GLA_KW), BF16),
            pltpu.VMEM((seq // GLA_CHUNK, 8, GLA_KW), F32),
            pltpu.VMEM((seq, GLA_WIDTH), F32),
            pltpu.VMEM((GLA_HEADS // 2, 2 * GLA_V_DIM, 2 * GLA_K_DIM), F32),
        ],
        compiler_params=pltpu.CompilerParams(
            dimension_semantics=("arbitrary",), vmem_limit_bytes=VMEM_LIMIT),
        name="gla",
    )(g3, lg3, norm_w)


def _mlp_kernel(tiles_per_seq, x_ref, od_ref, og_ref, wo_ref, ln2_ref, wup_ref, cw_ref, cb_ref,
                wdn_ref, lnf_ref, o_ref, x1_ref, h_ref, u_ref, act_ref, carry_ref):
    tm = x_ref.shape[0]
    halo = 8
    ck = FFN_CHUNK
    nch = D_FF // ck

    @pl.when(pl.program_id(0) % tiles_per_seq == 0)
    def _():
        carry_ref[...] = jnp.zeros_like(carry_ref)

    mix = jnp.concatenate([od_ref[...], og_ref[...]], axis=1)
    x1 = x_ref[...] + _dot(mix, wo_ref[...])
    x1_ref[...] = x1
    h_ref[...] = _rms(x1, ln2_ref[...]).astype(BF16)

    def gate_value(ref, c):
        return jnp.concatenate([ref[:, c * ck:(c + 1) * ck],
                                ref[:, D_FF + c * ck:D_FF + (c + 1) * ck]], axis=1)

    def up(c):
        par = c % 2
        u = _dot(h_ref[...], gate_value(wup_ref, c))
        u_ref[par, 0:halo, :] = carry_ref[c]
        u_ref[par, halo:halo + tm, :] = u
        carry_ref[c] = u[tm - halo:tm, :]

    def conv_act(c):
        par = c % 2
        w = gate_value(cw_ref, c)
        y = (u_ref[par, halo:halo + tm, :] * w[2:3, :]
             + u_ref[par, halo - 1:halo - 1 + tm, :] * w[1:2, :]
             + u_ref[par, halo - 2:halo - 2 + tm, :] * w[0:1, :]
             + gate_value(cb_ref, c))
        gate, val = y[:, :ck], y[:, ck:]
        act_ref[:, c * ck:(c + 1) * ck] = (gate / (1.0 + jnp.exp(-gate)) * val).astype(BF16)

    def down(c0, c1):
        return _dot(act_ref[:, c0 * ck:c1 * ck], wdn_ref[c0 * ck:c1 * ck, :])

    groups = [(g, min(g + DOWN_GROUP, nch)) for g in range(0, nch, DOWN_GROUP)]
    up(0)
    for c in range(nch):
        if c + 1 < nch:
            up(c + 1)
        for g0, g1 in groups[:-1]:
            if g1 == c:
                x1_ref[...] += down(g0, g1)
        conv_act(c)
    o_ref[...] = _rms(x1_ref[...] + down(*groups[-1]), lnf_ref[...])


def _mlp(x2, od, og, wo, ln2_w, wup, cw, cb, wdn, lnf_w, seq):
    n = x2.shape[0]
    tm = TM_FFN
    nch = D_FF // FFN_CHUNK
    tok = lambda w: pl.BlockSpec((tm, w), lambda i: (i, 0))
    return pl.pallas_call(
        functools.partial(_mlp_kernel, seq // tm),
        grid=(n // tm,),
        in_specs=[tok(D_MODEL), tok(DIFF_WIDTH), tok(GLA_WIDTH),
                  _const_spec(wo.shape), _const_spec((1, D_MODEL)),
                  _const_spec(wup.shape), _const_spec(cw.shape), _const_spec(cb.shape),
                  _const_spec(wdn.shape), _const_spec((1, D_MODEL))],
        out_specs=tok(D_MODEL),
        out_shape=jax.ShapeDtypeStruct((n, D_MODEL), F32),
        scratch_shapes=[
            pltpu.VMEM((tm, D_MODEL), F32),
            pltpu.VMEM((tm, D_MODEL), BF16),
            pltpu.VMEM((2, tm + 8, 2 * FFN_CHUNK), F32),
            pltpu.VMEM((tm, D_FF), BF16),
            pltpu.VMEM((nch, 8, 2 * FFN_CHUNK), F32),
        ],
        compiler_params=pltpu.CompilerParams(
            dimension_semantics=("arbitrary",), vmem_limit_bytes=VMEM_LIMIT),
        name="mlp",
    )(x2, od, og, wo, ln2_w, wup, cw, cb, wdn, lnf_w)


def kernel(x, ln1_w, w_in, diff_lq1, diff_lk1, diff_lq2, diff_lk2, diff_subln_w,
           gla_wg2, gla_bg, gla_norm_w, w_out, ln2_w, w_up, conv_w, conv_b,
           w_down, lnf_w):
    b, seq, d = x.shape
    n = b * seq
    depth = w_in.shape[0]
    assert depth == 1, depth
    x2 = x.reshape(n, d)
    qk_w = DIFF_HEADS * 2 * DIFF_QK_DIM
    c0 = 2 * qk_w + DIFF_WIDTH
    c1 = c0 + 2 * GLA_KW + GLA_WIDTH
    c2 = c1 + GLA_GATE_RANK
    for l in range(depth):
        lam_init = 0.8 - 0.6 * math.exp(-0.3 * l)
        wt = jnp.swapaxes(w_in[l], 0, 1)
        scale = jnp.ones((c0 + IN_G + IN_R, 1), F32)
        scale = scale.at[:qk_w].set(DIFF_QK_DIM ** -0.5).at[c0:c0 + GLA_KW].set(GLA_K_DIM ** -0.5)
        w_all = (jnp.concatenate(
            [wt[:c1], wt[c2:], wt[c1:c2], jnp.zeros((IN_R - GLA_GATE_RANK, d), F32)], axis=0)
            * scale).astype(BF16)
        wg2 = jnp.pad(gla_wg2[l], ((0, IN_R - GLA_GATE_RANK), (0, 0))).astype(BF16)

        a, g, lg, w_up16, w_down16, w_out16 = _in_proj(
            x2, ln1_w[l][None, :], w_all, wg2, gla_bg[l][None, :], w_up[l], w_down[l], w_out[l])
        o_diff = _diff_attn(a.reshape(b, seq, -1),
                            diff_lq1[l][None, :], diff_lk1[l][None, :],
                            diff_lq2[l][None, :], diff_lk2[l][None, :],
                            diff_subln_w[l][:, None], lam_init)
        o_gla = _gla(g.reshape(b, seq, -1), lg.reshape(b, seq, -1), gla_norm_w[l][None, :])

        x2 = _mlp(x2, o_diff.reshape(n, -1), o_gla.reshape(n, -1), w_out16,
                  ln2_w[l][None, :], w_up16, conv_w[l], conv_b[l][None, :],
                  w_down16, lnf_w[None, :], seq)
    return x2.reshape(b, seq, d)
```

```python
import functools
import math

import jax
import jax.numpy as jnp
from jax import lax
from jax.experimental import pallas as pl
from jax.experimental.pallas import tpu as pltpu

F32 = jnp.float32
BF16 = jnp.bfloat16

D_MODEL = 1024
DIFF_HEADS = 4
DIFF_QK_DIM = 64
DIFF_V_DIM = 128
DIFF_WIDTH = DIFF_HEADS * DIFF_V_DIM
GLA_HEADS = 4
GLA_K_DIM = 64
GLA_V_DIM = 128
GLA_KW = GLA_HEADS * GLA_K_DIM
GLA_WIDTH = GLA_HEADS * GLA_V_DIM
GLA_GATE_RANK = 16
GLA_TAU = 16.0
GLA_CHUNK = 64
GLA_BLK = 256
D_FF = 2816
EPS = 1e-6

LANES = 128
VMEM_LIMIT = 56 * 1024 * 1024

TM_PROJ = 1024
TM_FFN = 512
FFN_CHUNK = 1408
DOWN_GROUP = 1
assert D_FF % FFN_CHUNK == 0 and FFN_CHUNK % LANES == 0
TQ = 256
TK = 256
ROW_BLK = 64
ATTN_HEADS_PER_STEP = 2


def _dot(a, b):
    return jnp.dot(a, b, preferred_element_type=F32)


def _dot_nt(a, b):
    return lax.dot_general(a, b, (((1,), (1,)), ((), ())), preferred_element_type=F32)


def _dot_tn(a, b):
    return lax.dot_general(a, b, (((0,), (0,)), ((), ())), preferred_element_type=F32)


def _rms(x, w):
    ms = jnp.mean(x * x, axis=-1, keepdims=True)
    return x * lax.rsqrt(ms + EPS) * w


def _silu(x):
    h = 0.5 * x
    return h + h * jnp.tanh(h)


def _const_spec(shape):
    nd = len(shape)
    return pl.BlockSpec(shape, lambda *_: (0,) * nd, pipeline_mode=pl.Buffered(1))


IN_A = 2 * DIFF_HEADS * 2 * DIFF_QK_DIM + DIFF_WIDTH
IN_G = 2 * GLA_KW + 2 * GLA_WIDTH
IN_R = LANES


def _in_proj_kernel(x_ref, ln_ref, w_ref, wg2_ref, bg_ref, wup_ref, wdn_ref, wo_ref,
                    a_ref, g_ref, lg_ref, wup16_ref, wdn16_ref, wo16_ref):
    h = _rms(x_ref[...], ln_ref[...]).astype(BF16)
    a_ref[...] = _dot_nt(h, w_ref[0:IN_A, :]).astype(BF16)
    gr = _dot_nt(h, w_ref[IN_A:IN_A + IN_G + IN_R, :])
    g_ref[...] = gr[:, 0:IN_G].astype(BF16)
    r = gr[:, IN_G:IN_G + IN_R].astype(BF16)
    logits = _dot(r, wg2_ref[...]) + bg_ref[...]
    ls = jnp.minimum(logits, 0.0) - jnp.log(1.0 + jnp.exp(-jnp.abs(logits)))
    lg_ref[...] = ls * (1.0 / GLA_TAU)
    wup16_ref[...] = wup_ref[...].astype(BF16)
    wdn16_ref[...] = wdn_ref[...].astype(BF16)
    wo16_ref[...] = wo_ref[...].astype(BF16)


def _in_proj(x2, ln1_w, w_all, wg2, bg, w_up, w_down, w_out):
    n = x2.shape[0]
    tm = TM_PROJ
    steps = n // tm
    up_rows = w_up.shape[0] // steps
    out_rows = w_out.shape[0] // steps
    dn_rows = 2 * w_down.shape[0] // steps
    assert up_rows * steps == w_up.shape[0] and out_rows * steps == w_out.shape[0]
    assert dn_rows * steps == 2 * w_down.shape[0]
    assert up_rows % 16 == 0 and out_rows % 16 == 0 and dn_rows % 16 == 0
    slab = lambda rows, w, every: pl.BlockSpec((rows, w), lambda i: (i // every, 0))
    w_specs = [slab(up_rows, w_up.shape[1], 1), slab(dn_rows, w_down.shape[1], 2),
               slab(out_rows, w_out.shape[1], 1)]
    return pl.pallas_call(
        _in_proj_kernel,
        grid=(steps,),
        in_specs=[
            pl.BlockSpec((tm, D_MODEL), lambda i: (i, 0)),
            _const_spec((1, D_MODEL)),
            _const_spec(w_all.shape),
            _const_spec(wg2.shape),
            _const_spec((1, GLA_KW)),
        ] + w_specs,
        out_specs=[
            pl.BlockSpec((tm, IN_A), lambda i: (i, 0)),
            pl.BlockSpec((tm, IN_G), lambda i: (i, 0)),
            pl.BlockSpec((tm, GLA_KW), lambda i: (i, 0)),
        ] + w_specs,
        out_shape=[
            jax.ShapeDtypeStruct((n, IN_A), BF16),
            jax.ShapeDtypeStruct((n, IN_G), BF16),
            jax.ShapeDtypeStruct((n, GLA_KW), F32),
            jax.ShapeDtypeStruct(w_up.shape, BF16),
            jax.ShapeDtypeStruct(w_down.shape, BF16),
            jax.ShapeDtypeStruct(w_out.shape, BF16),
        ],
        compiler_params=pltpu.CompilerParams(
            dimension_semantics=("arbitrary",), vmem_limit_bytes=VMEM_LIMIT),
        name="in_proj",
    )(x2, ln1_w, w_all, wg2, bg, w_up, w_down, w_out)


def _diff_attn_kernel(lam_init, lq1_ref, lk1_ref, lq2_ref, lk2_ref, sw_ref,
                      q_ref, k_ref, v_ref, o_ref,
                      vt_ref, bias_ref, s_ref4, p_ref4):
    assert TQ == TK
    seq = q_ref.shape[1]
    nq = seq // TQ
    nk = seq // TK

    lam = (jnp.exp(jnp.sum(lq1_ref[...] * lk1_ref[...], axis=-1, keepdims=True))
           - jnp.exp(jnp.sum(lq2_ref[...] * lk2_ref[...], axis=-1, keepdims=True))
           + lam_init)

    hd = 2 * DIFF_QK_DIM
    heads = q_ref.shape[2] // hd
    for hh in range(heads):
        for t in range(nk):
            vt_ref[hh, :, t * TK:(t + 1) * TK] = (
                v_ref[0, t * TK:(t + 1) * TK, hh * DIFF_V_DIM:(hh + 1) * DIFF_V_DIM]
                .astype(F32).T.astype(BF16))

    row = lax.broadcasted_iota(jnp.int32, (2 * DIFF_QK_DIM, TQ), 0)
    krow = lax.broadcasted_iota(jnp.int32, (TQ, 2 * TQ), 0)
    qcol = lax.broadcasted_iota(jnp.int32, (TQ, 2 * TQ), 1) % TQ
    bias_ref[...] = jnp.where(krow <= qcol, 0.0, -jnp.inf).astype(F32)

    def fold8(x, op):
        parts = [x[a * 8:(a + 1) * 8, :] for a in range(x.shape[0] // 8)]
        while len(parts) > 1:
            parts = [op(parts[a], parts[a + 1]) for a in range(0, len(parts), 2)]
        return parts[0]

    def scores(w):
        hh, i = divmod(w, nq)
        nkeys = (i + 1) * TQ
        qt = q_ref[0, i * TQ:(i + 1) * TQ, hh * hd:(hh + 1) * hd].astype(F32).T.astype(BF16)
        zero = jnp.zeros_like(qt)
        q12t = jnp.concatenate([jnp.where(row < DIFF_QK_DIM, qt, zero),
                                jnp.where(row >= DIFF_QK_DIM, qt, zero)], axis=1)
        s_ref4[w % 2, 0:nkeys, :] = _dot(k_ref[0, 0:nkeys, hh * hd:(hh + 1) * hd], q12t)

    def softmax(w):
        i = w % nq
        nblk = (i + 1) * TQ // ROW_BLK

        def block(r):
            blk = s_ref4[w % 2, r * ROW_BLK:(r + 1) * ROW_BLK, :]
            d = r * ROW_BLK - i * TQ
            if d >= 0:
                blk = blk + bias_ref[d:d + ROW_BLK, :]
            return blk

        m8 = fold8(block(0), jnp.maximum)
        for r in range(1, nblk):
            m8 = jnp.maximum(m8, fold8(block(r), jnp.maximum))
        m = jnp.max(m8, axis=0, keepdims=True)
        mb = jnp.broadcast_to(m, (ROW_BLK, 2 * TQ))
        l8 = jnp.zeros((8, 2 * TQ), F32)
        for r in range(nblk):
            p = jnp.exp(block(r) - mb)
            l8 = l8 + fold8(p, jnp.add)
            p_ref4[w % 2, r * ROW_BLK:(r + 1) * ROW_BLK, :] = p.astype(BF16)
        return jnp.sum(l8, axis=0, keepdims=True)

    def values(w, l):
        hh, i = divmod(w, nq)
        nkeys = (i + 1) * TQ
        acc = _dot(vt_ref[hh, :, 0:nkeys], p_ref4[w % 2, 0:nkeys, :]) / l
        ot = acc[:, :TQ] - lam * acc[:, TQ:]
        ms = jnp.mean(ot * ot, axis=0, keepdims=True)
        ot = ot * lax.rsqrt(ms + EPS) * sw_ref[...] * (1.0 - lam_init)
        o_ref[0, i * TQ:(i + 1) * TQ, hh * DIFF_V_DIM:(hh + 1) * DIFF_V_DIM] = ot.T.astype(BF16)

    nwork = heads * nq
    scores(0)
    sums = {}
    for w in range(nwork):
        if w > 0:
            values(w - 1, sums.pop(w - 1))
        if w + 1 < nwork:
            scores(w + 1)
        sums[w] = softmax(w)
    values(nwork - 1, sums.pop(nwork - 1))


def _diff_attn(a3, lq1, lk1, lq2, lk2, subln_col, lam_init):
    b, seq, _ = a3.shape
    hd = 2 * DIFF_QK_DIM
    assert hd == DIFF_V_DIM
    hps = ATTN_HEADS_PER_STEP
    groups = DIFF_HEADS // hps
    vec = _const_spec((1, DIFF_QK_DIM))
    return pl.pallas_call(
        functools.partial(_diff_attn_kernel, lam_init),
        grid=(b, groups),
        in_specs=[
            vec, vec, vec, vec,
            _const_spec((DIFF_V_DIM, 1)),
            pl.BlockSpec((1, seq, hps * hd), lambda bi, g: (bi, 0, g)),
            pl.BlockSpec((1, seq, hps * hd), lambda bi, g: (bi, 0, groups + g)),
            pl.BlockSpec((1, seq, hps * DIFF_V_DIM), lambda bi, g: (bi, 0, 2 * groups + g)),
        ],
        out_specs=pl.BlockSpec((1, seq, hps * DIFF_V_DIM), lambda bi, g: (bi, 0, g)),
        out_shape=jax.ShapeDtypeStruct((b, seq, DIFF_WIDTH), BF16),
        scratch_shapes=[
            pltpu.VMEM((hps, DIFF_V_DIM, seq), BF16),
            pltpu.VMEM((TQ, 2 * TQ), F32),
            pltpu.VMEM((2, seq, 2 * TQ), F32),
            pltpu.VMEM((2, seq, 2 * TQ), BF16),
        ],
        compiler_params=pltpu.CompilerParams(
            dimension_semantics=("arbitrary", "arbitrary"), vmem_limit_bytes=VMEM_LIMIT),
        name="diff_attn",
    )(lq1, lk1, lq2, lk2, subln_col, a3, a3, a3)


def _gla_kernel(g_ref, lg_ref, nw_ref, o_ref, qd_ref, ke_ref, dec_ref, oi_ref, st_ref):
    c = GLA_CHUNK
    blk = GLA_BLK
    cpb = blk // c
    seq = g_ref.shape[1]
    kw, vw, dv = GLA_KW, GLA_WIDTH, GLA_V_DIM
    pw = 2 * GLA_K_DIM
    assert pw == LANES

    ri = lax.broadcasted_iota(jnp.int32, (blk, blk), 0)
    ci = lax.broadcasted_iota(jnp.int32, (blk, blk), 1)
    intra = (ri // c == ci // c) & (ri >= ci)
    tri = intra.astype(BF16)
    row_half = lax.broadcasted_iota(jnp.int32, (pw, blk), 0) // GLA_K_DIM
    st_diag = (lax.broadcasted_iota(jnp.int32, (2 * dv, pw), 0) // dv
               == lax.broadcasted_iota(jnp.int32, (2 * dv, pw), 1) // GLA_K_DIM)

    def pair(h):
        p = h // 2
        return slice(p * pw, (p + 1) * pw)

    def vcols(h):
        return slice(h * dv, (h + 1) * dv)

    st_ref[...] = jnp.zeros_like(st_ref)

    def span(start, size):
        if isinstance(start, int):
            return slice(start, start + size)
        return pl.ds(pl.multiple_of(start, size), size)

    def prep(bi):
        rows = span(bi * blk, blk)
        lg = lg_ref[0, rows, :]
        hi = lg.astype(BF16)
        lo = (lg - hi.astype(F32)).astype(BF16)
        b = _dot(tri, hi) + _dot(tri, lo)
        tot = jnp.concatenate(
            [jnp.broadcast_to(b[j * c + c - 1:(j + 1) * c, :], (c, kw)) for j in range(cpb)], axis=0)
        g = g_ref[0, rows, :]
        q = g[:, 0:kw].astype(F32)
        k = g[:, kw:2 * kw].astype(F32)
        qd = (q * jnp.exp(b)).astype(BF16)
        kd = k * jnp.exp(-b)
        ke = (k * jnp.exp(tot - b)).astype(BF16)
        qd_ref[rows, :] = qd
        ke_ref[rows, :] = ke
        dec = jnp.exp(tot)
        for j in range(cpb):
            dec_ref[bi * cpb + j] = dec[j * c:j * c + 8, :]
        kdt = kd.T.astype(BF16)
        for h in range(GLA_HEADS):
            kdt_h = jnp.where(row_half == h % 2, kdt[pair(h), :], jnp.zeros((), BF16))
            a = jnp.where(intra, _dot(qd[:, pair(h)], kdt_h), 0.0).astype(BF16)
            oi_ref[rows, vcols(h)] = _dot(a, g[:, 2 * kw + h * dv:2 * kw + (h + 1) * dv])

    def scan(n):
        rows = span(n * c, c)
        for p in range(GLA_HEADS // 2):
            lanes = slice(p * pw, (p + 1) * pw)
            vc = slice(2 * p * dv, 2 * (p + 1) * dv)
            st = st_ref[p]
            oi_ref[rows, vc] += _dot_nt(qd_ref[rows, lanes], st.astype(BF16))
            v_p = g_ref[0, rows, 2 * kw + 2 * p * dv:2 * kw + 2 * (p + 1) * dv]
            cs = jnp.where(st_diag, _dot_tn(v_p, ke_ref[rows, lanes]), 0.0)
            st_ref[p] = st * dec_ref[n, 0:1, lanes] + cs

    def scan_block(bi):
        for j in range(cpb):
            scan(bi * cpb + j)

    def finish(bi):
        rows = span(bi * blk, blk)
        nw = nw_ref[...]
        for h in range(GLA_HEADS):
            gate = g_ref[0, rows, 2 * kw + vw + h * dv:2 * kw + vw + (h + 1) * dv].astype(F32)
            y = _rms(oi_ref[rows, vcols(h)], nw) * _silu(gate)
            o_ref[0, rows, vcols(h)] = y.astype(BF16)

    nblk = seq // blk
    prep(0)
    for bi in range(nblk):
        if bi + 1 < nblk:
            prep(bi + 1)
        scan_block(bi)
        finish(bi)


def _gla(g3, lg3, norm_w):
    b, seq, gw = g3.shape
    return pl.pallas_call(
        _gla_kernel,
        grid=(b,),
        in_specs=[
            pl.BlockSpec((1, seq, gw), lambda bi: (bi, 0, 0)),
            pl.BlockSpec((1, seq, GLA_KW), lambda bi: (bi, 0, 0)),
            _const_spec((1, GLA_V_DIM)),
        ],
        out_specs=pl.BlockSpec((1, seq, GLA_WIDTH), lambda bi: (bi, 0, 0)),
        out_shape=jax.ShapeDtypeStruct((b, seq, GLA_WIDTH), BF16),
        scratch_shapes=[
            pltpu.VMEM((seq, GLA_KW), BF16),
            pltpu.VMEM((seq, GLA_KW), BF16),
            pltpu.VMEM((seq // GLA_CHUNK, 8, GLA_KW), F32),
            pltpu.VMEM((seq, GLA_WIDTH), F32),
            pltpu.VMEM((GLA_HEADS // 2, 2 * GLA_V_DIM, 2 * GLA_K_DIM), F32),
        ],
        compiler_params=pltpu.CompilerParams(
            dimension_semantics=("arbitrary",), vmem_limit_bytes=VMEM_LIMIT),
        name="gla",
    )(g3, lg3, norm_w)


def _mlp_kernel(tiles_per_seq, x_ref, od_ref, og_ref, wo_ref, ln2_ref, wup_ref, cw_ref, cb_ref,
                wdn_ref, lnf_ref, o_ref, x1_ref, h_ref, u_ref, act_ref, carry_ref):
    tm = x_ref.shape[0]
    halo = 8
    ck = FFN_CHUNK
    nch = D_FF // ck

    @pl.when(pl.program_id(0) % tiles_per_seq == 0)
    def _():
        carry_ref[...] = jnp.zeros_like(carry_ref)

    mix = jnp.concatenate([od_ref[...], og_ref[...]], axis=1)
    x1 = x_ref[...] + _dot(mix, wo_ref[...])
    x1_ref[...] = x1
    h_ref[...] = _rms(x1, ln2_ref[...]).astype(BF16)

    def gate_value(ref, c):
        return jnp.concatenate([ref[:, c * ck:(c + 1) * ck],
                                ref[:, D_FF + c * ck:D_FF + (c + 1) * ck]], axis=1)

    def up(c):
        par = c % 2
        u = _dot(h_ref[...], gate_value(wup_ref, c))
        u_ref[par, 0:halo, :] = carry_ref[c]
        u_ref[par, halo:halo + tm, :] = u
        carry_ref[c] = u[tm - halo:tm, :]

    def conv_act(c):
        par = c % 2
        w = gate_value(cw_ref, c)
        y = (u_ref[par, halo:halo + tm, :] * w[2:3, :]
             + u_ref[par, halo - 1:halo - 1 + tm, :] * w[1:2, :]
             + u_ref[par, halo - 2:halo - 2 + tm, :] * w[0:1, :]
             + gate_value(cb_ref, c))
        gate, val = y[:, :ck], y[:, ck:]
        act_ref[:, c * ck:(c + 1) * ck] = (_silu(gate) * val).astype(BF16)

    def down(c0, c1):
        return _dot(act_ref[:, c0 * ck:c1 * ck], wdn_ref[c0 * ck:c1 * ck, :])

    groups = [(g, min(g + DOWN_GROUP, nch)) for g in range(0, nch, DOWN_GROUP)]
    up(0)
    for c in range(nch):
        if c + 1 < nch:
            up(c + 1)
        for g0, g1 in groups[:-1]:
            if g1 == c:
                x1_ref[...] += down(g0, g1)
        conv_act(c)
    o_ref[...] = _rms(x1_ref[...] + down(*groups[-1]), lnf_ref[...])


def _mlp(x2, od, og, wo, ln2_w, wup, cw, cb, wdn, lnf_w, seq):
    n = x2.shape[0]
    tm = TM_FFN
    nch = D_FF // FFN_CHUNK
    tok = lambda w: pl.BlockSpec((tm, w), lambda i: (i, 0))
    return pl.pallas_call(
        functools.partial(_mlp_kernel, seq // tm),
        grid=(n // tm,),
        in_specs=[tok(D_MODEL), tok(DIFF_WIDTH), tok(GLA_WIDTH),
                  _const_spec(wo.shape), _const_spec((1, D_MODEL)),
                  _const_spec(wup.shape), _const_spec(cw.shape), _const_spec(cb.shape),
                  _const_spec(wdn.shape), _const_spec((1, D_MODEL))],
        out_specs=tok(D_MODEL),
        out_shape=jax.ShapeDtypeStruct((n, D_MODEL), F32),
        scratch_shapes=[
            pltpu.VMEM((tm, D_MODEL), F32),
            pltpu.VMEM((tm, D_MODEL), BF16),
            pltpu.VMEM((2, tm + 8, 2 * FFN_CHUNK), F32),
            pltpu.VMEM((tm, D_FF), BF16),
            pltpu.VMEM((nch, 8, 2 * FFN_CHUNK), F32),
        ],
        compiler_params=pltpu.CompilerParams(
            dimension_semantics=("arbitrary",), vmem_limit_bytes=VMEM_LIMIT),
        name="mlp",
    )(x2, od, og, wo, ln2_w, wup, cw, cb, wdn, lnf_w)


def kernel(x, ln1_w, w_in, diff_lq1, diff_lk1, diff_lq2, diff_lk2, diff_subln_w,
           gla_wg2, gla_bg, gla_norm_w, w_out, ln2_w, w_up, conv_w, conv_b,
           w_down, lnf_w):
    b, seq, d = x.shape
    n = b * seq
    depth = w_in.shape[0]
    assert depth == 1, depth
    x2 = x.reshape(n, d)
    qk_w = DIFF_HEADS * 2 * DIFF_QK_DIM
    c0 = 2 * qk_w + DIFF_WIDTH
    c1 = c0 + 2 * GLA_KW + GLA_WIDTH
    c2 = c1 + GLA_GATE_RANK
    for l in range(depth):
        lam_init = 0.8 - 0.6 * math.exp(-0.3 * l)
        wt = jnp.swapaxes(w_in[l], 0, 1)
        scale = jnp.ones((c0 + IN_G + IN_R, 1), F32)
        scale = scale.at[:qk_w].set(DIFF_QK_DIM ** -0.5).at[c0:c0 + GLA_KW].set(GLA_K_DIM ** -0.5)
        w_all = (jnp.concatenate(
            [wt[:c1], wt[c2:], wt[c1:c2], jnp.zeros((IN_R - GLA_GATE_RANK, d), F32)], axis=0)
            * scale).astype(BF16)
        wg2 = jnp.pad(gla_wg2[l], ((0, IN_R - GLA_GATE_RANK), (0, 0))).astype(BF16)

        a, g, lg, w_up16, w_down16, w_out16 = _in_proj(
            x2, ln1_w[l][None, :], w_all, wg2, gla_bg[l][None, :], w_up[l], w_down[l], w_out[l])
        o_diff = _diff_attn(a.reshape(b, seq, -1),
                            diff_lq1[l][None, :], diff_lk1[l][None, :],
                            diff_lq2[l][None, :], diff_lk2[l][None, :],
                            diff_subln_w[l][:, None], lam_init)
        o_gla = _gla(g.reshape(b, seq, -1), lg.reshape(b, seq, -1), gla_norm_w[l][None, :])

        x2 = _mlp(x2, o_diff.reshape(n, -1), o_gla.reshape(n, -1), w_out16,
                  ln2_w[l][None, :], w_up16, conv_w[l], conv_b[l][None, :],
                  w_down16, lnf_w[None, :], seq)
    return x2.reshape(b, seq, d)
```

```python
import functools
import math

import jax
import jax.numpy as jnp
from jax import lax
from jax.experimental import pallas as pl
from jax.experimental.pallas import tpu as pltpu

F32 = jnp.float32
BF16 = jnp.bfloat16

D_MODEL = 1024
DIFF_HEADS = 4
DIFF_QK_DIM = 64
DIFF_V_DIM = 128
DIFF_WIDTH = DIFF_HEADS * DIFF_V_DIM
GLA_HEADS = 4
GLA_K_DIM = 64
GLA_V_DIM = 128
GLA_KW = GLA_HEADS * GLA_K_DIM
GLA_WIDTH = GLA_HEADS * GLA_V_DIM
GLA_GATE_RANK = 16
GLA_TAU = 16.0
GLA_CHUNK = 64
GLA_BLK = 256
D_FF = 2816
EPS = 1e-6

LANES = 128
VMEM_LIMIT = 56 * 1024 * 1024

TM_PROJ = 1024
TM_FFN = 512
FFN_CHUNK = 1408
DOWN_GROUP = 1
assert D_FF % FFN_CHUNK == 0 and FFN_CHUNK % LANES == 0
TQ = 256
TK = 256
ROW_BLK = 64
ATTN_HEADS_PER_STEP = 2


def _dot(a, b):
    return jnp.dot(a, b, preferred_element_type=F32)


def _dot_nt(a, b):
    return lax.dot_general(a, b, (((1,), (1,)), ((), ())), preferred_element_type=F32)


def _dot_tn(a, b):
    return lax.dot_general(a, b, (((0,), (0,)), ((), ())), preferred_element_type=F32)


def _rms(x, w):
    ms = jnp.mean(x * x, axis=-1, keepdims=True)
    return x * lax.rsqrt(ms + EPS) * w


def _silu(x):
    h = 0.5 * x
    return h + h * jnp.tanh(h)


def _const_spec(shape):
    nd = len(shape)
    return pl.BlockSpec(shape, lambda *_: (0,) * nd, pipeline_mode=pl.Buffered(1))


IN_A = 2 * DIFF_HEADS * 2 * DIFF_QK_DIM + DIFF_WIDTH
IN_G = 2 * GLA_KW + 2 * GLA_WIDTH
IN_R = LANES


def _in_proj_kernel(x_ref, ln_ref, w_ref, wg2_ref, bg_ref, wup_ref, wdn_ref, wo_ref,
                    a_ref, g_ref, lg_ref, wup16_ref, wdn16_ref, wo16_ref):
    h = _rms(x_ref[...], ln_ref[...]).astype(BF16)
    a_ref[...] = _dot_nt(h, w_ref[0:IN_A, :]).astype(BF16)
    gr = _dot_nt(h, w_ref[IN_A:IN_A + IN_G + IN_R, :])
    g_ref[...] = gr[:, 0:IN_G].astype(BF16)
    r = gr[:, IN_G:IN_G + IN_R].astype(BF16)
    logits = _dot(r, wg2_ref[...]) + bg_ref[...]
    ls = jnp.minimum(logits, 0.0) - jnp.log(1.0 + jnp.exp(-jnp.abs(logits)))
    lg_ref[...] = ls * (1.0 / GLA_TAU)
    wup16_ref[...] = wup_ref[...].astype(BF16)
    wdn16_ref[...] = wdn_ref[...].astype(BF16)
    wo16_ref[...] = wo_ref[...].astype(BF16)


def _in_proj(x2, ln1_w, w_all, wg2, bg, w_up, w_down, w_out):
    n = x2.shape[0]
    tm = TM_PROJ
    steps = n // tm
    up_rows = w_up.shape[0] // steps
    out_rows = w_out.shape[0] // steps
    dn_rows = 2 * w_down.shape[0] // steps
    assert up_rows * steps == w_up.shape[0] and out_rows * steps == w_out.shape[0]
    assert dn_rows * steps == 2 * w_down.shape[0]
    assert up_rows % 16 == 0 and out_rows % 16 == 0 and dn_rows % 16 == 0
    slab = lambda rows, w, every: pl.BlockSpec((rows, w), lambda i: (i // every, 0))
    w_specs = [slab(up_rows, w_up.shape[1], 1), slab(dn_rows, w_down.shape[1], 2),
               slab(out_rows, w_out.shape[1], 1)]
    return pl.pallas_call(
        _in_proj_kernel,
        grid=(steps,),
        in_specs=[
            pl.BlockSpec((tm, D_MODEL), lambda i: (i, 0)),
            _const_spec((1, D_MODEL)),
            _const_spec(w_all.shape),
            _const_spec(wg2.shape),
            _const_spec((1, GLA_KW)),
        ] + w_specs,
        out_specs=[
            pl.BlockSpec((tm, IN_A), lambda i: (i, 0)),
            pl.BlockSpec((tm, IN_G), lambda i: (i, 0)),
            pl.BlockSpec((tm, GLA_KW), lambda i: (i, 0)),
        ] + w_specs,
        out_shape=[
            jax.ShapeDtypeStruct((n, IN_A), BF16),
            jax.ShapeDtypeStruct((n, IN_G), BF16),
            jax.ShapeDtypeStruct((n, GLA_KW), F32),
            jax.ShapeDtypeStruct(w_up.shape, BF16),
            jax.ShapeDtypeStruct(w_down.shape, BF16),
            jax.ShapeDtypeStruct(w_out.shape, BF16),
        ],
        compiler_params=pltpu.CompilerParams(
            dimension_semantics=("arbitrary",), vmem_limit_bytes=VMEM_LIMIT),
        name="in_proj",
    )(x2, ln1_w, w_all, wg2, bg, w_up, w_down, w_out)


def _diff_attn_kernel(lam_init, lq1_ref, lk1_ref, lq2_ref, lk2_ref, sw_ref,
                      q_ref, k_ref, v_ref, o_ref,
                      vt_ref, bias_ref, s_ref4, p_ref4):
    assert TQ == TK
    seq = q_ref.shape[1]
    nq = seq // TQ
    nk = seq // TK

    lam = (jnp.exp(jnp.sum(lq1_ref[...] * lk1_ref[...], axis=-1, keepdims=True))
           - jnp.exp(jnp.sum(lq2_ref[...] * lk2_ref[...], axis=-1, keepdims=True))
           + lam_init)

    hd = 2 * DIFF_QK_DIM
    heads = q_ref.shape[2] // hd
    for hh in range(heads):
        for t in range(nk):
            vt_ref[hh, :, t * TK:(t + 1) * TK] = (
                v_ref[0, t * TK:(t + 1) * TK, hh * DIFF_V_DIM:(hh + 1) * DIFF_V_DIM]
                .astype(F32).T.astype(BF16))

    row = lax.broadcasted_iota(jnp.int32, (2 * DIFF_QK_DIM, TQ), 0)
    krow = lax.broadcasted_iota(jnp.int32, (TQ, 2 * TQ), 0)
    qcol = lax.broadcasted_iota(jnp.int32, (TQ, 2 * TQ), 1) % TQ
    bias_ref[...] = jnp.where(krow <= qcol, 0.0, -jnp.inf).astype(F32)

    def fold8(x, op):
        parts = [x[a * 8:(a + 1) * 8, :] for a in range(x.shape[0] // 8)]
        while len(parts) > 1:
            parts = [op(parts[a], parts[a + 1]) for a in range(0, len(parts), 2)]
        return parts[0]

    def scores(w):
        hh, i = divmod(w, nq)
        nkeys = (i + 1) * TQ
        qt = q_ref[0, i * TQ:(i + 1) * TQ, hh * hd:(hh + 1) * hd].astype(F32).T.astype(BF16)
        zero = jnp.zeros_like(qt)
        q12t = jnp.concatenate([jnp.where(row < DIFF_QK_DIM, qt, zero),
                                jnp.where(row >= DIFF_QK_DIM, qt, zero)], axis=1)
        s_ref4[w % 2, 0:nkeys, :] = _dot(k_ref[0, 0:nkeys, hh * hd:(hh + 1) * hd], q12t)

    def softmax(w):
        i = w % nq
        nblk = (i + 1) * TQ // ROW_BLK

        def block(r):
            blk = s_ref4[w % 2, r * ROW_BLK:(r + 1) * ROW_BLK, :]
            d = r * ROW_BLK - i * TQ
            if d >= 0:
                blk = blk + bias_ref[d:d + ROW_BLK, :]
            return blk

        m8 = fold8(block(0), jnp.maximum)
        for r in range(1, nblk):
            m8 = jnp.maximum(m8, fold8(block(r), jnp.maximum))
        m = jnp.max(m8, axis=0, keepdims=True)
        mb = jnp.broadcast_to(m, (ROW_BLK, 2 * TQ))
        l8 = jnp.zeros((8, 2 * TQ), F32)
        for r in range(nblk):
            p = jnp.exp(block(r) - mb)
            l8 = l8 + fold8(p, jnp.add)
            p_ref4[w % 2, r * ROW_BLK:(r + 1) * ROW_BLK, :] = p.astype(BF16)
        return jnp.sum(l8, axis=0, keepdims=True)

    def values(w, l):
        hh, i = divmod(w, nq)
        nkeys = (i + 1) * TQ
        acc = _dot(vt_ref[hh, :, 0:nkeys], p_ref4[w % 2, 0:nkeys, :]) / l
        ot = acc[:, :TQ] - lam * acc[:, TQ:]
        ms = jnp.mean(ot * ot, axis=0, keepdims=True)
        ot = ot * lax.rsqrt(ms + EPS) * sw_ref[...] * (1.0 - lam_init)
        o_ref[0, i * TQ:(i + 1) * TQ, hh * DIFF_V_DIM:(hh + 1) * DIFF_V_DIM] = ot.T.astype(BF16)

    nwork = heads * nq
    scores(0)
    sums = {}
    for w in range(nwork):
        if w > 0:
            values(w - 1, sums.pop(w - 1))
        if w + 1 < nwork:
            scores(w + 1)
        sums[w] = softmax(w)
    values(nwork - 1, sums.pop(nwork - 1))


def _diff_attn(a3, lq1, lk1, lq2, lk2, subln_col, lam_init):
    b, seq, _ = a3.shape
    hd = 2 * DIFF_QK_DIM
    assert hd == DIFF_V_DIM
    hps = ATTN_HEADS_PER_STEP
    groups = DIFF_HEADS // hps
    vec = _const_spec((1, DIFF_QK_DIM))
    return pl.pallas_call(
        functools.partial(_diff_attn_kernel, lam_init),
        grid=(b, groups),
        in_specs=[
            vec, vec, vec, vec,
            _const_spec((DIFF_V_DIM, 1)),
            pl.BlockSpec((1, seq, hps * hd), lambda bi, g: (bi, 0, g)),
            pl.BlockSpec((1, seq, hps * hd), lambda bi, g: (bi, 0, groups + g)),
            pl.BlockSpec((1, seq, hps * DIFF_V_DIM), lambda bi, g: (bi, 0, 2 * groups + g)),
        ],
        out_specs=pl.BlockSpec((1, seq, hps * DIFF_V_DIM), lambda bi, g: (bi, 0, g)),
        out_shape=jax.ShapeDtypeStruct((b, seq, DIFF_WIDTH), BF16),
        scratch_shapes=[
            pltpu.VMEM((hps, DIFF_V_DIM, seq), BF16),
            pltpu.VMEM((TQ, 2 * TQ), F32),
            pltpu.VMEM((2, seq, 2 * TQ), F32),
            pltpu.VMEM((2, seq, 2 * TQ), BF16),
        ],
        compiler_params=pltpu.CompilerParams(
            dimension_semantics=("arbitrary", "arbitrary"), vmem_limit_bytes=VMEM_LIMIT),
        name="diff_attn",
    )(lq1, lk1, lq2, lk2, subln_col, a3, a3, a3)


def _gla_kernel(g_ref, lg_ref, nw_ref, o_ref, qd_ref, ke_ref, dec_ref, oi_ref, st_ref):
    c = GLA_CHUNK
    blk = GLA_BLK
    cpb = blk // c
    seq = g_ref.shape[1]
    kw, vw, dv = GLA_KW, GLA_WIDTH, GLA_V_DIM
    pw = 2 * GLA_K_DIM
    assert pw == LANES

    ri = lax.broadcasted_iota(jnp.int32, (blk, blk), 0)
    ci = lax.broadcasted_iota(jnp.int32, (blk, blk), 1)
    intra = (ri // c == ci // c) & (ri >= ci)
    tri = intra.astype(BF16)
    row_half = lax.broadcasted_iota(jnp.int32, (pw, blk), 0) // GLA_K_DIM
    st_diag = (lax.broadcasted_iota(jnp.int32, (2 * dv, pw), 0) // dv
               == lax.broadcasted_iota(jnp.int32, (2 * dv, pw), 1) // GLA_K_DIM)

    def pair(h):
        p = h // 2
        return slice(p * pw, (p + 1) * pw)

    def vcols(h):
        return slice(h * dv, (h + 1) * dv)

    st_ref[...] = jnp.zeros_like(st_ref)

    def span(start, size):
        if isinstance(start, int):
            return slice(start, start + size)
        return pl.ds(pl.multiple_of(start, size), size)

    def prep(bi):
        rows = span(bi * blk, blk)
        lg = lg_ref[0, rows, :]
        hi = lg.astype(BF16)
        lo = (lg - hi.astype(F32)).astype(BF16)
        b = _dot(tri, hi) + _dot(tri, lo)
        tot = jnp.concatenate(
            [jnp.broadcast_to(b[j * c + c - 1:(j + 1) * c, :], (c, kw)) for j in range(cpb)], axis=0)
        g = g_ref[0, rows, :]
        q = g[:, 0:kw].astype(F32)
        k = g[:, kw:2 * kw].astype(F32)
        qd = (q * jnp.exp(b)).astype(BF16)
        kd = k * jnp.exp(-b)
        ke = (k * jnp.exp(tot - b)).astype(BF16)
        qd_ref[rows, :] = qd
        ke_ref[rows, :] = ke
        dec = jnp.exp(tot)
        for j in range(cpb):
            dec_ref[bi * cpb + j] = dec[j * c:j * c + 8, :]
        kdt = kd.T.astype(BF16)
        for h in range(GLA_HEADS):
            kdt_h = jnp.where(row_half == h % 2, kdt[pair(h), :], jnp.zeros((), BF16))
            a = jnp.where(intra, _dot(qd[:, pair(h)], kdt_h), 0.0).astype(BF16)
            oi_ref[rows, vcols(h)] = _dot(a, g[:, 2 * kw + h * dv:2 * kw + (h + 1) * dv])

    def scan(n):
        rows = span(n * c, c)
        for p in range(GLA_HEADS // 2):
            lanes = slice(p * pw, (p + 1) * pw)
            vc = slice(2 * p * dv, 2 * (p + 1) * dv)
            st = st_ref[p]
            oi_ref[rows, vc] += _dot_nt(qd_ref[rows, lanes], st.astype(BF16))
            v_p = g_ref[0, rows, 2 * kw + 2 * p * dv:2 * kw + 2 * (p + 1) * dv]
            cs = jnp.where(st_diag, _dot_tn(v_p, ke_ref[rows, lanes]), 0.0)
            st_ref[p] = st * dec_ref[n, 0:1, lanes] + cs

    def scan_block(bi):
        for j in range(cpb):
            scan(bi * cpb + j)

    def finish(bi):
        rows = span(bi * blk, blk)
        nw = nw_ref[...]
        for h in range(GLA_HEADS):
            gate = g_ref[0, rows, 2 * kw + vw + h * dv:2 * kw + vw + (h + 1) * dv].astype(F32)
            y = _rms(oi_ref[rows, vcols(h)], nw) * _silu(gate)
            o_ref[0, rows, vcols(h)] = y.astype(BF16)

    nblk = seq // blk
    prep(0)
    for bi in range(nblk):
        if bi + 1 < nblk:
            prep(bi + 1)
        scan_block(bi)
        finish(bi)


def _gla(g3, lg3, norm_w):
    b, seq, gw = g3.shape
    return pl.pallas_call(
        _gla_kernel,
        grid=(b,),
        in_specs=[
            pl.BlockSpec((1, seq, gw), lambda bi: (bi, 0, 0)),
            pl.BlockSpec((1, seq, GLA_KW), lambda bi: (bi, 0, 0)),
            _const_spec((1, GLA_V_DIM)),
        ],
        out_specs=pl.BlockSpec((1, seq, GLA_WIDTH), lambda bi: (bi, 0, 0)),
        out_shape=jax.ShapeDtypeStruct((b, seq, GLA_WIDTH), BF16),
        scratch_shapes=[
            pltpu.VMEM((seq, GLA_KW), BF16),
            pltpu.VMEM((seq, GLA_KW), BF16),
            pltpu.VMEM((seq // GLA_CHUNK, 8, GLA_KW), F32),
            pltpu.VMEM((seq, GLA_WIDTH), F32),
            pltpu.VMEM((GLA_HEADS // 2, 2 * GLA_V_DIM, 2 * GLA_K_DIM), F32),
        ],
        compiler_params=pltpu.CompilerParams(
            dimension_semantics=("arbitrary",), vmem_limit_bytes=VMEM_LIMIT),
        name="gla",
    )(g3, lg3, norm_w)


def _mlp_kernel(tiles_per_seq, x_ref, od_ref, og_ref, wo_ref, ln2_ref, wup_ref, cw_ref, cb_ref,
                wdn_ref, lnf_ref, o_ref, x1_ref, h_ref, u_ref, act_ref, carry_ref):
    tm = x_ref.shape[0]
    halo = 8
    ck = FFN_CHUNK
    nch = D_FF // ck

    @pl.when(pl.program_id(0) % tiles_per_seq == 0)
    def _():
        carry_ref[...] = jnp.zeros_like(carry_ref)

    mix = jnp.concatenate([od_ref[...], og_ref[...]], axis=1)
    x1 = x_ref[...] + _dot(mix, wo_ref[...])
    x1_ref[...] = x1
    h_ref[...] = _rms(x1, ln2_ref[...]).astype(BF16)

    def gate_value(ref, c):
        return jnp.concatenate([ref[:, c * ck:(c + 1) * ck],
                                ref[:, D_FF + c * ck:D_FF + (c + 1) * ck]], axis=1)

    def up(c):
        par = c % 2
        u = _dot(h_ref[...], gate_value(wup_ref, c))
        u_ref[par, 0:halo, :] = carry_ref[c]
        u_ref[par, halo:halo + tm, :] = u
        carry_ref[c] = u[tm - halo:tm, :]

    def conv_act(c):
        par = c % 2
        w = gate_value(cw_ref, c)
        y = (u_ref[par, halo:halo + tm, :] * w[2:3, :]
             + u_ref[par, halo - 1:halo - 1 + tm, :] * w[1:2, :]
             + u_ref[par, halo - 2:halo - 2 + tm, :] * w[0:1, :]
             + gate_value(cb_ref, c))
        gate, val = y[:, :ck], y[:, ck:]
        act_ref[:, c * ck:(c + 1) * ck] = (gate / (1.0 + jnp.exp(-gate)) * val).astype(BF16)

    def down(c0, c1):
        return _dot(act_ref[:, c0 * ck:c1 * ck], wdn_ref[c0 * ck:c1 * ck, :])

    groups = [(g, min(g + DOWN_GROUP, nch)) for g in range(0, nch, DOWN_GROUP)]
    up(0)
    for c in range(nch):
        if c + 1 < nch:
            up(c + 1)
        for g0, g1 in groups[:-1]:
            if g1 == c:
                x1_ref[...] += down(g0, g1)
        conv_act(c)
    o_ref[...] = _rms(x1_ref[...] + down(*groups[-1]), lnf_ref[...])


def _mlp(x2, od, og, wo, ln2_w, wup, cw, cb, wdn, lnf_w, seq):
    n = x2.shape[0]
    tm = TM_FFN
    nch = D_FF // FFN_CHUNK
    tok = lambda w: pl.BlockSpec((tm, w), lambda i: (i, 0))
    return pl.pallas_call(
        functools.partial(_mlp_kernel, seq // tm),
        grid=(n // tm,),
        in_specs=[tok(D_MODEL), tok(DIFF_WIDTH), tok(GLA_WIDTH),
                  _const_spec(wo.shape), _const_spec((1, D_MODEL)),
                  _const_spec(wup.shape), _const_spec(cw.shape), _const_spec(cb.shape),
                  _const_spec(wdn.shape), _const_spec((1, D_MODEL))],
        out_specs=tok(D_MODEL),
        out_shape=jax.ShapeDtypeStruct((n, D_MODEL), F32),
        scratch_shapes=[
            pltpu.VMEM((tm, D_MODEL), F32),
            pltpu.VMEM((tm, D_MODEL), BF16),
            pltpu.VMEM((2, tm + 8, 2 * FFN_CHUNK), F32),
            pltpu.VMEM((tm, D_FF), BF16),
            pltpu.VMEM((nch, 8, 2 * FFN_CHUNK), F32),
        ],
        compiler_params=pltpu.CompilerParams(
            dimension_semantics=("arbitrary",), vmem_limit_bytes=VMEM_LIMIT),
        name="mlp",
    )(x2, od, og, wo, ln2_w, wup, cw, cb, wdn, lnf_w)


def kernel(x, ln1_w, w_in, diff_lq1, diff_lk1, diff_lq2, diff_lk2, diff_subln_w,
           gla_wg2, gla_bg, gla_norm_w, w_out, ln2_w, w_up, conv_w, conv_b,
           w_down, lnf_w):
    b, seq, d = x.shape
    n = b * seq
    depth = w_in.shape[0]
    assert depth == 1, depth
    x2 = x.reshape(n, d)
    qk_w = DIFF_HEADS * 2 * DIFF_QK_DIM
    c0 = 2 * qk_w + DIFF_WIDTH
    c1 = c0 + 2 * GLA_KW + GLA_WIDTH
    c2 = c1 + GLA_GATE_RANK
    for l in range(depth):
        lam_init = 0.8 - 0.6 * math.exp(-0.3 * l)
        wt = jnp.swapaxes(w_in[l], 0, 1)
        scale = jnp.ones((c0 + IN_G + IN_R, 1), F32)
        scale = scale.at[:qk_w].set(DIFF_QK_DIM ** -0.5).at[c0:c0 + GLA_KW].set(GLA_K_DIM ** -0.5)
        w_all = (jnp.concatenate(
            [wt[:c1], wt[c2:], wt[c1:c2], jnp.zeros((IN_R - GLA_GATE_RANK, d), F32)], axis=0)
            * scale).astype(BF16)
        wg2 = jnp.pad(gla_wg2[l], ((0, IN_R - GLA_GATE_RANK), (0, 0))).astype(BF16)

        a, g, lg, w_up16, w_down16, w_out16 = _in_proj(
            x2, ln1_w[l][None, :], w_all, wg2, gla_bg[l][None, :], w_up[l], w_down[l], w_out[l])
        o_diff = _diff_attn(a.reshape(b, seq, -1),
                            diff_lq1[l][None, :], diff_lk1[l][None, :],
                            diff_lq2[l][None, :], diff_lk2[l][None, :],
                            diff_subln_w[l][:, None], lam_init)
        o_gla = _gla(g.reshape(b, seq, -1), lg.reshape(b, seq, -1), gla_norm_w[l][None, :])

        x2 = _mlp(x2, o_diff.reshape(n, -1), o_gla.reshape(n, -1), w_out16,
                  ln2_w[l][None, :], w_up16, conv_w[l], conv_b[l][None, :],
                  w_down16, lnf_w[None, :], seq)
    return x2.reshape(b, seq, d)
```

```python
import functools
import math

import jax
import jax.numpy as jnp
from jax import lax
from jax.experimental import pallas as pl
from jax.experimental.pallas import tpu as pltpu

F32 = jnp.float32
BF16 = jnp.bfloat16

D_MODEL = 1024
DIFF_HEADS = 4
DIFF_QK_DIM = 64
DIFF_V_DIM = 128
DIFF_WIDTH = DIFF_HEADS * DIFF_V_DIM
GLA_HEADS = 4
GLA_K_DIM = 64
GLA_V_DIM = 128
GLA_KW = GLA_HEADS * GLA_K_DIM
GLA_WIDTH = GLA_HEADS * GLA_V_DIM
GLA_GATE_RANK = 16
GLA_TAU = 16.0
GLA_CHUNK = 64
GLA_BLK = 256
D_FF = 2816
EPS = 1e-6

LANES = 128
VMEM_LIMIT = 56 * 1024 * 1024

TM_PROJ = 1024
TM_FFN = 512
FFN_CHUNK = 1408
DOWN_GROUP = 1
assert D_FF % FFN_CHUNK == 0 and FFN_CHUNK % LANES == 0
TQ = 256
TK = 256
ROW_BLK = 64
ATTN_HEADS_PER_STEP = 2


def _dot(a, b):
    return jnp.dot(a, b, preferred_element_type=F32)


def _dot_nt(a, b):
    return lax.dot_general(a, b, (((1,), (1,)), ((), ())), preferred_element_type=F32)


def _dot_tn(a, b):
    return lax.dot_general(a, b, (((0,), (0,)), ((), ())), preferred_element_type=F32)


def _rms(x, w):
    ms = jnp.mean(x * x, axis=-1, keepdims=True)
    return x * lax.rsqrt(ms + EPS) * w


def _silu(x):
    h = 0.5 * x
    return h + h * jnp.tanh(h)


def _const_spec(shape):
    nd = len(shape)
    return pl.BlockSpec(shape, lambda *_: (0,) * nd, pipeline_mode=pl.Buffered(1))


IN_A = 2 * DIFF_HEADS * 2 * DIFF_QK_DIM + DIFF_WIDTH
IN_G = 2 * GLA_KW + 2 * GLA_WIDTH
IN_R = LANES


def _in_proj_kernel(x_ref, ln_ref, w_ref, wg2_ref, bg_ref, wup_ref, wdn_ref, wo_ref,
                    a_ref, g_ref, lg_ref, wup16_ref, wdn16_ref, wo16_ref):
    h = _rms(x_ref[...], ln_ref[...]).astype(BF16)
    a_ref[...] = _dot_nt(h, w_ref[0:IN_A, :]).astype(BF16)
    gr = _dot_nt(h, w_ref[IN_A:IN_A + IN_G + IN_R, :])
    g_ref[...] = gr[:, 0:IN_G].astype(BF16)
    r = gr[:, IN_G:IN_G + IN_R].astype(BF16)
    logits = _dot(r, wg2_ref[...]) + bg_ref[...]
    ls = jnp.minimum(logits, 0.0) - jnp.log(1.0 + jnp.exp(-jnp.abs(logits)))
    lg_ref[...] = ls * (1.0 / GLA_TAU)
    wup16_ref[...] = wup_ref[...].astype(BF16)
    wdn16_ref[...] = wdn_ref[...].astype(BF16)
    wo16_ref[...] = wo_ref[...].astype(BF16)


def _in_proj(x2, ln1_w, w_all, wg2, bg, w_up, w_down, w_out):
    n = x2.shape[0]
    tm = TM_PROJ
    steps = n // tm
    up_rows = w_up.shape[0] // steps
    out_rows = w_out.shape[0] // steps
    dn_rows = 2 * w_down.shape[0] // steps
    assert up_rows * steps == w_up.shape[0] and out_rows * steps == w_out.shape[0]
    assert dn_rows * steps == 2 * w_down.shape[0]
    assert up_rows % 16 == 0 and out_rows % 16 == 0 and dn_rows % 16 == 0
    slab = lambda rows, w, every: pl.BlockSpec((rows, w), lambda i: (i // every, 0))
    w_specs = [slab(up_rows, w_up.shape[1], 1), slab(dn_rows, w_down.shape[1], 2),
               slab(out_rows, w_out.shape[1], 1)]
    return pl.pallas_call(
        _in_proj_kernel,
        grid=(steps,),
        in_specs=[
            pl.BlockSpec((tm, D_MODEL), lambda i: (i, 0)),
            _const_spec((1, D_MODEL)),
            _const_spec(w_all.shape),
            _const_spec(wg2.shape),
            _const_spec((1, GLA_KW)),
        ] + w_specs,
        out_specs=[
            pl.BlockSpec((tm, IN_A), lambda i: (i, 0)),
            pl.BlockSpec((tm, IN_G), lambda i: (i, 0)),
            pl.BlockSpec((tm, GLA_KW), lambda i: (i, 0)),
        ] + w_specs,
        out_shape=[
            jax.ShapeDtypeStruct((n, IN_A), BF16),
            jax.ShapeDtypeStruct((n, IN_G), BF16),
            jax.ShapeDtypeStruct((n, GLA_KW), F32),
            jax.ShapeDtypeStruct(w_up.shape, BF16),
            jax.ShapeDtypeStruct(w_down.shape, BF16),
            jax.ShapeDtypeStruct(w_out.shape, BF16),
        ],
        compiler_params=pltpu.CompilerParams(
            dimension_semantics=("arbitrary",), vmem_limit_bytes=VMEM_LIMIT),
        name="in_proj",
    )(x2, ln1_w, w_all, wg2, bg, w_up, w_down, w_out)


def _diff_attn_kernel(lam_init, lq1_ref, lk1_ref, lq2_ref, lk2_ref, sw_ref,
                      q_ref, k_ref, v_ref, o_ref,
                      vt_ref, bias_ref, s_ref4, p_ref4):
    assert TQ == TK
    seq = q_ref.shape[1]
    nq = seq // TQ
    nk = seq // TK

    lam = (jnp.exp(jnp.sum(lq1_ref[...] * lk1_ref[...], axis=-1, keepdims=True))
           - jnp.exp(jnp.sum(lq2_ref[...] * lk2_ref[...], axis=-1, keepdims=True))
           + lam_init)

    hd = 2 * DIFF_QK_DIM
    heads = q_ref.shape[2] // hd
    for hh in range(heads):
        for t in range(nk):
            vt_ref[hh, :, t * TK:(t + 1) * TK] = (
                v_ref[0, t * TK:(t + 1) * TK, hh * DIFF_V_DIM:(hh + 1) * DIFF_V_DIM]
                .astype(F32).T.astype(BF16))

    row = lax.broadcasted_iota(jnp.int32, (2 * DIFF_QK_DIM, TQ), 0)
    krow = lax.broadcasted_iota(jnp.int32, (TQ, 2 * TQ), 0)
    qcol = lax.broadcasted_iota(jnp.int32, (TQ, 2 * TQ), 1) % TQ
    bias_ref[...] = jnp.where(krow <= qcol, 0.0, -jnp.inf).astype(F32)

    def fold8(x, op):
        parts = [x[a * 8:(a + 1) * 8, :] for a in range(x.shape[0] // 8)]
        while len(parts) > 1:
            parts = [op(parts[a], parts[a + 1]) for a in range(0, len(parts), 2)]
        return parts[0]

    def scores(w):
        hh, i = divmod(w, nq)
        nkeys = (i + 1) * TQ
        qt = q_ref[0, i * TQ:(i + 1) * TQ, hh * hd:(hh + 1) * hd].astype(F32).T.astype(BF16)
        zero = jnp.zeros_like(qt)
        q12t = jnp.concatenate([jnp.where(row < DIFF_QK_DIM, qt, zero),
                                jnp.where(row >= DIFF_QK_DIM, qt, zero)], axis=1)
        s_ref4[w % 2, 0:nkeys, :] = _dot(k_ref[0, 0:nkeys, hh * hd:(hh + 1) * hd], q12t)

    def softmax(w):
        i = w % nq
        nblk = (i + 1) * TQ // ROW_BLK

        def block(r):
            blk = s_ref4[w % 2, r * ROW_BLK:(r + 1) * ROW_BLK, :]
            d = r * ROW_BLK - i * TQ
            if d >= 0:
                blk = blk + bias_ref[d:d + ROW_BLK, :]
            return blk

        m8 = fold8(block(0), jnp.maximum)
        for r in range(1, nblk):
            m8 = jnp.maximum(m8, fold8(block(r), jnp.maximum))
        m = jnp.max(m8, axis=0, keepdims=True)
        mb = jnp.broadcast_to(m, (ROW_BLK, 2 * TQ))
        l8 = jnp.zeros((8, 2 * TQ), F32)
        for r in range(nblk):
            p = jnp.exp(block(r) - mb)
            l8 = l8 + fold8(p, jnp.add)
            p_ref4[w % 2, r * ROW_BLK:(r + 1) * ROW_BLK, :] = p.astype(BF16)
        return jnp.sum(l8, axis=0, keepdims=True)

    def values(w, l):
        hh, i = divmod(w, nq)
        nkeys = (i + 1) * TQ
        acc = _dot(vt_ref[hh, :, 0:nkeys], p_ref4[w % 2, 0:nkeys, :]) / l
        ot = acc[:, :TQ] - lam * acc[:, TQ:]
        ms = jnp.mean(ot * ot, axis=0, keepdims=True)
        ot = ot * lax.rsqrt(ms + EPS) * sw_ref[...] * (1.0 - lam_init)
        o_ref[0, i * TQ:(i + 1) * TQ, hh * DIFF_V_DIM:(hh + 1) * DIFF_V_DIM] = ot.T.astype(BF16)

    nwork = heads * nq
    scores(0)
    sums = {}
    for w in range(nwork):
        if w > 0:
            values(w - 1, sums.pop(w - 1))
        if w + 1 < nwork:
            scores(w + 1)
        sums[w] = softmax(w)
    values(nwork - 1, sums.pop(nwork - 1))


def _diff_attn(a3, lq1, lk1, lq2, lk2, subln_col, lam_init):
    b, seq, _ = a3.shape
    hd = 2 * DIFF_QK_DIM
    assert hd == DIFF_V_DIM
    hps = ATTN_HEADS_PER_STEP
    groups = DIFF_HEADS // hps
    vec = _const_spec((1, DIFF_QK_DIM))
    return pl.pallas_call(
        functools.partial(_diff_attn_kernel, lam_init),
        grid=(b, groups),
        in_specs=[
            vec, vec, vec, vec,
            _const_spec((DIFF_V_DIM, 1)),
            pl.BlockSpec((1, seq, hps * hd), lambda bi, g: (bi, 0, g)),
            pl.BlockSpec((1, seq, hps * hd), lambda bi, g: (bi, 0, groups + g)),
            pl.BlockSpec((1, seq, hps * DIFF_V_DIM), lambda bi, g: (bi, 0, 2 * groups + g)),
        ],
        out_specs=pl.BlockSpec((1, seq, hps * DIFF_V_DIM), lambda bi, g: (bi, 0, g)),
        out_shape=jax.ShapeDtypeStruct((b, seq, DIFF_WIDTH), BF16),
        scratch_shapes=[
            pltpu.VMEM((hps, DIFF_V_DIM, seq), BF16),
            pltpu.VMEM((TQ, 2 * TQ), F32),
            pltpu.VMEM((2, seq, 2 * TQ), F32),
            pltpu.VMEM((2, seq, 2 * TQ), BF16),
        ],
        compiler_params=pltpu.CompilerParams(
            dimension_semantics=("arbitrary", "arbitrary"), vmem_limit_bytes=VMEM_LIMIT),
        name="diff_attn",
    )(lq1, lk1, lq2, lk2, subln_col, a3, a3, a3)


def _gla_kernel(g_ref, lg_ref, nw_ref, o_ref, qd_ref, ke_ref, dec_ref, oi_ref, st_ref):
    c = GLA_CHUNK
    blk = GLA_BLK
    cpb = blk // c
    seq = g_ref.shape[1]
    kw, vw, dv = GLA_KW, GLA_WIDTH, GLA_V_DIM
    pw = 2 * GLA_K_DIM
    assert pw == LANES

    ri = lax.broadcasted_iota(jnp.int32, (blk, blk), 0)
    ci = lax.broadcasted_iota(jnp.int32, (blk, blk), 1)
    intra = (ri // c == ci // c) & (ri >= ci)
    tri = intra.astype(BF16)
    ri2 = lax.broadcasted_iota(jnp.int32, (blk, 2 * blk), 0)
    ci2 = lax.broadcasted_iota(jnp.int32, (blk, 2 * blk), 1) % blk
    intra2 = (ri2 // c == ci2 // c) & (ri2 >= ci2)
    row_half = lax.broadcasted_iota(jnp.int32, (pw, blk), 0) // GLA_K_DIM
    st_diag = (lax.broadcasted_iota(jnp.int32, (2 * dv, pw), 0) // dv
               == lax.broadcasted_iota(jnp.int32, (2 * dv, pw), 1) // GLA_K_DIM)

    def pair(h):
        p = h // 2
        return slice(p * pw, (p + 1) * pw)

    def vcols(h):
        return slice(h * dv, (h + 1) * dv)

    st_ref[...] = jnp.zeros_like(st_ref)

    def span(start, size):
        if isinstance(start, int):
            return slice(start, start + size)
        return pl.ds(pl.multiple_of(start, size), size)

    def prep(bi):
        rows = span(bi * blk, blk)
        lg = lg_ref[0, rows, :]
        hi = lg.astype(BF16)
        lo = (lg - hi.astype(F32)).astype(BF16)
        b = _dot(tri, hi) + _dot(tri, lo)
        tot = jnp.concatenate(
            [jnp.broadcast_to(b[j * c + c - 1:(j + 1) * c, :], (c, kw)) for j in range(cpb)], axis=0)
        g = g_ref[0, rows, :]
        q = g[:, 0:kw].astype(F32)
        k = g[:, kw:2 * kw].astype(F32)
        qd = (q * jnp.exp(b)).astype(BF16)
        kd = k * jnp.exp(-b)
        ke = (k * jnp.exp(tot - b)).astype(BF16)
        qd_ref[rows, :] = qd
        ke_ref[rows, :] = ke
        dec = jnp.exp(tot)
        for j in range(cpb):
            dec_ref[bi * cpb + j] = dec[j * c:j * c + 8, :]
        kdt = kd.T.astype(BF16)
        bz = jnp.zeros((), BF16)
        for p in range(GLA_HEADS // 2):
            lanes = slice(p * pw, (p + 1) * pw)
            kp = kdt[lanes, :]
            rhs = jnp.concatenate([jnp.where(row_half == 0, kp, bz),
                                   jnp.where(row_half == 1, kp, bz)], axis=1)
            a = jnp.where(intra2, _dot(qd[:, lanes], rhs), 0.0).astype(BF16)
            v0 = g[:, 2 * kw + 2 * p * dv:2 * kw + (2 * p + 1) * dv]
            v1 = g[:, 2 * kw + (2 * p + 1) * dv:2 * kw + (2 * p + 2) * dv]
            vz = jnp.zeros_like(v0)
            vbd = jnp.concatenate([jnp.concatenate([v0, vz], axis=1),
                                   jnp.concatenate([vz, v1], axis=1)], axis=0)
            oi_ref[rows, 2 * p * dv:2 * (p + 1) * dv] = _dot(a, vbd)

    def scan(n):
        rows = span(n * c, c)
        for p in range(GLA_HEADS // 2):
            lanes = slice(p * pw, (p + 1) * pw)
            vc = slice(2 * p * dv, 2 * (p + 1) * dv)
            st = st_ref[p]
            oi_ref[rows, vc] += _dot_nt(qd_ref[rows, lanes], st.astype(BF16))
            v_p = g_ref[0, rows, 2 * kw + 2 * p * dv:2 * kw + 2 * (p + 1) * dv]
            cs = jnp.where(st_diag, _dot_tn(v_p, ke_ref[rows, lanes]), 0.0)
            st_ref[p] = st * dec_ref[n, 0:1, lanes] + cs

    def scan_block(bi):
        for j in range(cpb):
            scan(bi * cpb + j)

    def finish(bi):
        rows = span(bi * blk, blk)
        nw = nw_ref[...]
        for h in range(GLA_HEADS):
            gate = g_ref[0, rows, 2 * kw + vw + h * dv:2 * kw + vw + (h + 1) * dv].astype(F32)
            y = _rms(oi_ref[rows, vcols(h)], nw) * _silu(gate)
            o_ref[0, rows, vcols(h)] = y.astype(BF16)

    nblk = seq // blk
    prep(0)
    for bi in range(nblk):
        if bi + 1 < nblk:
            prep(bi + 1)
        scan_block(bi)
        finish(bi)


def _gla(g3, lg3, norm_w):
    b, seq, gw = g3.shape
    return pl.pallas_call(
        _gla_kernel,
        grid=(b,),
        in_specs=[
            pl.BlockSpec((1, seq, gw), lambda bi: (bi, 0, 0)),
            pl.BlockSpec((1, seq, GLA_KW), lambda bi: (bi, 0, 0)),
            _const_spec((1, GLA_V_DIM)),
        ],
        out_specs=pl.BlockSpec((1, seq, GLA_WIDTH), lambda bi: (bi, 0, 0)),
        out_shape=jax.ShapeDtypeStruct((b, seq, GLA_WIDTH), BF16),
        scratch_shapes=[
            pltpu.VMEM((seq, GLA_KW), BF16),
            pltpu.VMEM((seq, GLA_KW), BF16),
            pltpu.VMEM((seq // GLA_CHUNK, 8, GLA_KW), F32),
            pltpu.VMEM((seq, GLA_WIDTH), F32),
            pltpu.VMEM((GLA_HEADS // 2, 2 * GLA_V_DIM, 2 * GLA_K_DIM), F32),
        ],
        compiler_params=pltpu.CompilerParams(
            dimension_semantics=("arbitrary",), vmem_limit_bytes=VMEM_LIMIT),
        name="gla",
    )(g3, lg3, norm_w)


def _mlp_kernel(tiles_per_seq, x_ref, od_ref, og_ref, wo_ref, ln2_ref, wup_ref, cw_ref, cb_ref,
                wdn_ref, lnf_ref, o_ref, x1_ref, h_ref, u_ref, act_ref, carry_ref):
    tm = x_ref.shape[0]
    halo = 8
    ck = FFN_CHUNK
    nch = D_FF // ck

    @pl.when(pl.program_id(0) % tiles_per_seq == 0)
    def _():
        carry_ref[...] = jnp.zeros_like(carry_ref)

    mix = jnp.concatenate([od_ref[...], og_ref[...]], axis=1)
    x1 = x_ref[...] + _dot(mix, wo_ref[...])
    x1_ref[...] = x1
    h_ref[...] = _rms(x1, ln2_ref[...]).astype(BF16)

    def gate_value(ref, c):
        return jnp.concatenate([ref[:, c * ck:(c + 1) * ck],
                                ref[:, D_FF + c * ck:D_FF + (c + 1) * ck]], axis=1)

    def up(c):
        par = c % 2
        u = _dot(h_ref[...], gate_value(wup_ref, c))
        u_ref[par, 0:halo, :] = carry_ref[c]
        u_ref[par, halo:halo + tm, :] = u
        carry_ref[c] = u[tm - halo:tm, :]

    def conv_act(c):
        par = c % 2
        w = gate_value(cw_ref, c)
        y = (u_ref[par, halo:halo + tm, :] * w[2:3, :]
             + u_ref[par, halo - 1:halo - 1 + tm, :] * w[1:2, :]
             + u_ref[par, halo - 2:halo - 2 + tm, :] * w[0:1, :]
             + gate_value(cb_ref, c))
        gate, val = y[:, :ck], y[:, ck:]
        act_ref[:, c * ck:(c + 1) * ck] = (gate / (1.0 + jnp.exp(-gate)) * val).astype(BF16)

    def down(c0, c1):
        return _dot(act_ref[:, c0 * ck:c1 * ck], wdn_ref[c0 * ck:c1 * ck, :])

    groups = [(g, min(g + DOWN_GROUP, nch)) for g in range(0, nch, DOWN_GROUP)]
    up(0)
    for c in range(nch):
        if c + 1 < nch:
            up(c + 1)
        for g0, g1 in groups[:-1]:
            if g1 == c:
                x1_ref[...] += down(g0, g1)
        conv_act(c)
    o_ref[...] = _rms(x1_ref[...] + down(*groups[-1]), lnf_ref[...])


def _mlp(x2, od, og, wo, ln2_w, wup, cw, cb, wdn, lnf_w, seq):
    n = x2.shape[0]
    tm = TM_FFN
    nch = D_FF // FFN_CHUNK
    tok = lambda w: pl.BlockSpec((tm, w), lambda i: (i, 0))
    return pl.pallas_call(
        functools.partial(_mlp_kernel, seq // tm),
        grid=(n // tm,),
        in_specs=[tok(D_MODEL), tok(DIFF_WIDTH), tok(GLA_WIDTH),
                  _const_spec(wo.shape), _const_spec((1, D_MODEL)),
                  _const_spec(wup.shape), _const_spec(cw.shape), _const_spec(cb.shape),
                  _const_spec(wdn.shape), _const_spec((1, D_MODEL))],
        out_specs=tok(D_MODEL),
        out_shape=jax.ShapeDtypeStruct((n, D_MODEL), F32),
        scratch_shapes=[
            pltpu.VMEM((tm, D_MODEL), F32),
            pltpu.VMEM((tm, D_MODEL), BF16),
            pltpu.VMEM((2, tm + 8, 2 * FFN_CHUNK), F32),
            pltpu.VMEM((tm, D_FF), BF16),
            pltpu.VMEM((nch, 8, 2 * FFN_CHUNK), F32),
        ],
        compiler_params=pltpu.CompilerParams(
            dimension_semantics=("arbitrary",), vmem_limit_bytes=VMEM_LIMIT),
        name="mlp",
    )(x2, od, og, wo, ln2_w, wup, cw, cb, wdn, lnf_w)


def kernel(x, ln1_w, w_in, diff_lq1, diff_lk1, diff_lq2, diff_lk2, diff_subln_w,
           gla_wg2, gla_bg, gla_norm_w, w_out, ln2_w, w_up, conv_w, conv_b,
           w_down, lnf_w):
    b, seq, d = x.shape
    n = b * seq
    depth = w_in.shape[0]
    assert depth == 1, depth
    x2 = x.reshape(n, d)
    qk_w = DIFF_HEADS * 2 * DIFF_QK_DIM
    c0 = 2 * qk_w + DIFF_WIDTH
    c1 = c0 + 2 * GLA_KW + GLA_WIDTH
    c2 = c1 + GLA_GATE_RANK
    for l in range(depth):
        lam_init = 0.8 - 0.6 * math.exp(-0.3 * l)
        wt = jnp.swapaxes(w_in[l], 0, 1)
        scale = jnp.ones((c0 + IN_G + IN_R, 1), F32)
        scale = scale.at[:qk_w].set(DIFF_QK_DIM ** -0.5).at[c0:c0 + GLA_KW].set(GLA_K_DIM ** -0.5)
        w_all = (jnp.concatenate(
            [wt[:c1], wt[c2:], wt[c1:c2], jnp.zeros((IN_R - GLA_GATE_RANK, d), F32)], axis=0)
            * scale).astype(BF16)
        wg2 = jnp.pad(gla_wg2[l], ((0, IN_R - GLA_GATE_RANK), (0, 0))).astype(BF16)

        a, g, lg, w_up16, w_down16, w_out16 = _in_proj(
            x2, ln1_w[l][None, :], w_all, wg2, gla_bg[l][None, :], w_up[l], w_down[l], w_out[l])
        o_diff = _diff_attn(a.reshape(b, seq, -1),
                            diff_lq1[l][None, :], diff_lk1[l][None, :],
                            diff_lq2[l][None, :], diff_lk2[l][None, :],
                            diff_subln_w[l][:, None], lam_init)
        o_gla = _gla(g.reshape(b, seq, -1), lg.reshape(b, seq, -1), gla_norm_w[l][None, :])

        x2 = _mlp(x2, o_diff.reshape(n, -1), o_gla.reshape(n, -1), w_out16,
                  ln2_w[l][None, :], w_up16, conv_w[l], conv_b[l][None, :],
                  w_down16, lnf_w[None, :], seq)
    return x2.reshape(b, seq, d)
```
